```python
import jax, jax.numpy as jnp
from jax import lax
import numpy as np

D_MODEL = 1024
BATCH = 4
SEQ = 4096
DEPTH = 4
DEC_BATCH = 128
DEC_SEQ = 4
PAST_LEN = 8192
PAGE_SIZE = 128

HEAD_DIM = 64
FOX_HEADS = 8
FOX_KV_HEADS = 4
FOX_GROUP = FOX_HEADS // FOX_KV_HEADS
FOX_WIDTH = FOX_HEADS * HEAD_DIM
MLA_HEADS = 8
MLA_NOPE_DIM = 64
MLA_ROPE_DIM = 32
MLA_V_DIM = 64
MLA_Q_RANK = 256
MLA_KV_RANK = 128
MLA_WIDTH = MLA_HEADS * MLA_V_DIM
MIX_WIDTH = FOX_WIDTH + MLA_WIDTH
FQ_END = FOX_WIDTH
FK_END = FQ_END + FOX_KV_HEADS * HEAD_DIM
FV_END = FK_END + FOX_KV_HEADS * HEAD_DIM
FF_END = FV_END + FOX_HEADS
CQ_END = FF_END + MLA_Q_RANK
CKV_END = CQ_END + MLA_KV_RANK
IN_COLS = CKV_END + MLA_ROPE_DIM
D_FF_DENSE = 2816
N_EXPERTS = 8
TOP_K = 2
D_FF_EXPERT = 3584
ROPE_THETA = 10000.0
NORM_EPS = 1e-6
Q_BLOCK = 128
FORGET_BIAS_INIT = 3.0

kernel_name = 'fox_mla_parallel_heads_decoder_step'


def _rmsnorm(x, g):
    xf = x.astype(jnp.float32)
    y = xf * lax.rsqrt(jnp.mean(xf * xf, axis=-1, keepdims=True) + NORM_EPS)
    return (y * g.astype(jnp.float32)).astype(x.dtype)


def _rope(x, pos):
    half = x.shape[-1] // 2
    inv_freq = jnp.power(ROPE_THETA, -jnp.arange(half, dtype=jnp.float32) / half)
    ang = pos.astype(jnp.float32)[:, None] * inv_freq[None, :]
    cos, sin = jnp.cos(ang)[:, None, :], jnp.sin(ang)[:, None, :]
    xf = x.astype(jnp.float32)
    x1, x2 = xf[..., :half], xf[..., half:]
    return jnp.concatenate([x1 * cos - x2 * sin, x2 * cos + x1 * sin], axis=-1).astype(x.dtype)


def _q_block(sq):
    return Q_BLOCK if sq % Q_BLOCK == 0 else sq


def _split_blocks(a, qb):
    b, s = a.shape[:2]
    return jnp.moveaxis(a.reshape((b, s // qb, qb) + a.shape[2:]), 1, 0)


def _merge_blocks(a):
    a = jnp.moveaxis(a, 0, 1)
    return a.reshape((a.shape[0], a.shape[1] * a.shape[2]) + a.shape[3:])


def _gather_pages(pool, page_table):
    g = pool[page_table]
    return g.reshape((g.shape[0], g.shape[1] * g.shape[2]) + g.shape[3:])


def _project(hn, pos, w_in, b_f, g_cq, g_ckv, w_uq, w_uk):
    b, s, _ = hn.shape
    proj = jnp.einsum('bsd,dc->bsc', hn, w_in)
    fq, fk, fv, fz, cq, ckv, kr = jnp.split(proj, [FQ_END, FK_END, FV_END, FF_END, CQ_END, CKV_END], axis=-1)
    fq = fq.reshape(b, s, FOX_KV_HEADS, FOX_GROUP, HEAD_DIM)
    fk = fk.reshape(b, s, FOX_KV_HEADS, HEAD_DIM)
    fv = fv.reshape(b, s, FOX_KV_HEADS, HEAD_DIM)
    logf = jax.nn.log_sigmoid((fz + b_f).astype(jnp.float32))
    q = jnp.einsum('bsr,rhe->bshe', _rmsnorm(cq, g_cq), w_uq)
    q_nope, q_rope = q[..., :MLA_NOPE_DIM], q[..., MLA_NOPE_DIM:]
    q_lat = jnp.einsum('bshn,chn->bshc', q_nope, w_uk)
    q_rope = _rope(q_rope, pos)
    ckv = _rmsnorm(ckv, g_ckv)
    kr = _rope(kr[:, :, None, :], pos)[:, :, 0, :]
    return fq, fk, fv, logf, q_lat, q_rope, ckv, kr


def _fox_attention(q, k, v, logf_k, q_pos, k_pos):
    b, sq = q.shape[:2]
    qb = _q_block(sq)
    scale = HEAD_DIM ** -0.5
    cum = jnp.cumsum(logf_k.astype(jnp.float32), axis=1)
    ck = jnp.moveaxis(cum.reshape(b, -1, FOX_KV_HEADS, FOX_GROUP), 1, -1)
    cq = cum[:, cum.shape[1] - sq:]

    def one_block(args):
        qi, cqi, pi = args
        s = jnp.einsum('bqhgd,bkhd->bhgqk', qi, k, preferred_element_type=jnp.float32) * scale
        cqi = jnp.moveaxis(cqi.reshape(b, qb, FOX_KV_HEADS, FOX_GROUP), 1, -1)
        s = s + cqi[..., :, None] - ck[..., None, :]
        s = jnp.where(k_pos[None, :] <= pi[:, None], s, -jnp.inf)
        p = jax.nn.softmax(s, axis=-1).astype(v.dtype)
        return jnp.einsum('bhgqk,bkhd->bqhgd', p, v)

    out = lax.map(one_block, (_split_blocks(q, qb), _split_blocks(cq, qb), q_pos.reshape(-1, qb)))
    return _merge_blocks(out)


def _mla_attention(q_lat, q_rope, ckv, kr, q_pos, k_pos):
    sq = q_lat.shape[1]
    qb = _q_block(sq)
    scale = (MLA_NOPE_DIM + MLA_ROPE_DIM) ** -0.5

    def one_block(args):
        ql, qr, pi = args
        s = (jnp.einsum('bqhc,bkc->bhqk', ql, ckv, preferred_element_type=jnp.float32)
             + jnp.einsum('bqhr,bkr->bhqk', qr, kr, preferred_element_type=jnp.float32)) * scale
        s = jnp.where(k_pos[None, :] <= pi[:, None], s, -jnp.inf)
        p = jax.nn.softmax(s, axis=-1).astype(ckv.dtype)
        return jnp.einsum('bhqk,bkc->bqhc', p, ckv)

    out = lax.map(one_block, (_split_blocks(q_lat, qb), _split_blocks(q_rope, qb), q_pos.reshape(-1, qb)))
    return _merge_blocks(out)


def _merge_heads(fox_o, mla_lat, w_uv, g_fox_out, g_mla_out, w_o):
    b, s = fox_o.shape[:2]
    fox_o = _rmsnorm(fox_o.reshape(b, s, FOX_WIDTH), g_fox_out)
    mla_o = jnp.einsum('bshc,chd->bshd', mla_lat, w_uv).reshape(b, s, MLA_WIDTH)
    mla_o = _rmsnorm(mla_o, g_mla_out)
    return jnp.einsum('bsm,md->bsd', jnp.concatenate([fox_o, mla_o], axis=-1), w_o)


def _swiglu(h, w_gate, w_up, w_down):
    a = jax.nn.silu(jnp.einsum('bsd,df->bsf', h, w_gate)) * jnp.einsum('bsd,df->bsf', h, w_up)
    return jnp.einsum('bsf,fd->bsd', a, w_down)


def _moe_swiglu(h, w_router, w_gate, w_up, w_down):
    logits = jnp.einsum('bsd,de->bse', h, w_router, preferred_element_type=jnp.float32)
    top_v, top_i = lax.top_k(logits, TOP_K)
    top_w = jax.nn.softmax(top_v, axis=-1)
    gates = jnp.sum(top_w[..., None] * jax.nn.one_hot(top_i, N_EXPERTS, dtype=jnp.float32), axis=-2)
    out = jnp.zeros_like(h)
    for e in range(N_EXPERTS):
        out = out + (gates[..., e:e + 1] * _swiglu(h, w_gate[e], w_up[e], w_down[e])).astype(h.dtype)
    return out


def setup_inputs(seed: int = 0) -> dict:
    key = jax.random.key(seed)
    ks = iter(jax.random.split(key, 40))

    def nrm(shape, scale):
        return jax.random.normal(next(ks), shape, jnp.float32) * scale

    n_pages = PAST_LEN // PAGE_SIZE
    n_used = DEC_BATCH * n_pages
    n_pool = n_used + max(1, n_used // 4)
    n_dense = (DEPTH + 1) // 2
    n_moe = DEPTH // 2
    page_table = jax.random.permutation(next(ks), n_pool)[:n_used].reshape(DEC_BATCH, n_pages).astype(jnp.int32)
    return {
        'x_prompt': nrm((BATCH, SEQ, D_MODEL), 1.0),
        'x_sample': nrm((DEC_BATCH, DEC_SEQ, D_MODEL), 1.0),
        'cache_fox_k': nrm((DEPTH, n_pool, PAGE_SIZE, FOX_KV_HEADS, HEAD_DIM), 1.0),
        'cache_fox_v': nrm((DEPTH, n_pool, PAGE_SIZE, FOX_KV_HEADS, HEAD_DIM), 1.0),
        'cache_fox_logf': jax.nn.log_sigmoid(FORGET_BIAS_INIT + nrm((DEPTH, n_pool, PAGE_SIZE, FOX_HEADS), 1.0)),
        'cache_mla_ckv': nrm((DEPTH, n_pool, PAGE_SIZE, MLA_KV_RANK), 1.0),
        'cache_mla_krope': nrm((DEPTH, n_pool, PAGE_SIZE, MLA_ROPE_DIM), 1.0),
        'page_table': page_table,
        'w_in': nrm((DEPTH, D_MODEL, IN_COLS), D_MODEL ** -0.5),
        'b_f': FORGET_BIAS_INIT + nrm((DEPTH, FOX_HEADS), 0.1),
        'g_attn': 1.0 + nrm((DEPTH, D_MODEL), 0.02),
        'g_cq': 1.0 + nrm((DEPTH, MLA_Q_RANK), 0.02),
        'g_ckv': 1.0 + nrm((DEPTH, MLA_KV_RANK), 0.02),
        'w_uq': nrm((DEPTH, MLA_Q_RANK, MLA_HEADS, MLA_NOPE_DIM + MLA_ROPE_DIM), MLA_Q_RANK ** -0.5),
        'w_uk': nrm((DEPTH, MLA_KV_RANK, MLA_HEADS, MLA_NOPE_DIM), MLA_KV_RANK ** -0.5),
        'w_uv': nrm((DEPTH, MLA_KV_RANK, MLA_HEADS, MLA_V_DIM), MLA_KV_RANK ** -0.5),
        'g_fox_out': 1.0 + nrm((DEPTH, FOX_WIDTH), 0.02),
        'g_mla_out': 1.0 + nrm((DEPTH, MLA_WIDTH), 0.02),
        'w_o': nrm((DEPTH, MIX_WIDTH, D_MODEL), MIX_WIDTH ** -0.5),
        'g_ffn': 1.0 + nrm((DEPTH, D_MODEL), 0.02),
        'w_gate_dense': nrm((n_dense, D_MODEL, D_FF_DENSE), D_MODEL ** -0.5),
        'w_up_dense': nrm((n_dense, D_MODEL, D_FF_DENSE), D_MODEL ** -0.5),
        'w_down_dense': nrm((n_dense, D_FF_DENSE, D_MODEL), D_FF_DENSE ** -0.5),
        'w_router': nrm((n_moe, D_MODEL, N_EXPERTS), D_MODEL ** -0.5),
        'w_gate_exp': nrm((n_moe, N_EXPERTS, D_MODEL, D_FF_EXPERT), D_MODEL ** -0.5),
        'w_up_exp': nrm((n_moe, N_EXPERTS, D_MODEL, D_FF_EXPERT), D_MODEL ** -0.5),
        'w_down_exp': nrm((n_moe, N_EXPERTS, D_FF_EXPERT, D_MODEL), D_FF_EXPERT ** -0.5),
        'g_final': 1.0 + nrm((D_MODEL,), 0.02),
    }


def reference(x_prompt, x_sample, cache_fox_k, cache_fox_v, cache_fox_logf, cache_mla_ckv, cache_mla_krope,
              page_table, w_in, b_f, g_attn, g_cq, g_ckv, w_uq, w_uk, w_uv, g_fox_out, g_mla_out, w_o, g_ffn,
              w_gate_dense, w_up_dense, w_down_dense, w_router, w_gate_exp, w_up_exp, w_down_exp, g_final):
    seq_p = x_prompt.shape[1]
    dec_seq = x_sample.shape[1]
    past_len = page_table.shape[1] * cache_fox_k.shape[2]
    pos_p = jnp.arange(seq_p, dtype=jnp.int32)
    pos_s = past_len + jnp.arange(dec_seq, dtype=jnp.int32)
    pos_s_keys = jnp.arange(past_len + dec_seq, dtype=jnp.int32)

    yp, ys = x_prompt, x_sample
    p_k, p_v, p_f, p_c, p_r = [], [], [], [], []
    s_k, s_v, s_f, s_c, s_r = [], [], [], [], []
    for l in range(DEPTH):
        proj_w = (w_in[l], b_f[l], g_cq[l], g_ckv[l], w_uq[l], w_uk[l])
        out_w = (w_uv[l], g_fox_out[l], g_mla_out[l], w_o[l])

        fq, fk, fv, lf, ql, qr, ckv, kr = _project(_rmsnorm(yp, g_attn[l]), pos_p, *proj_w)
        fo = _fox_attention(fq, fk, fv, lf, pos_p, pos_p)
        mo = _mla_attention(ql, qr, ckv, kr, pos_p, pos_p)
        yp = yp + _merge_heads(fo, mo, *out_w)
        p_k.append(fk); p_v.append(fv); p_f.append(lf); p_c.append(ckv); p_r.append(kr)

        fq, fk, fv, lf, ql, qr, ckv, kr = _project(_rmsnorm(ys, g_attn[l]), pos_s, *proj_w)
        k_all = jnp.concatenate([_gather_pages(cache_fox_k[l], page_table), fk], axis=1)
        v_all = jnp.concatenate([_gather_pages(cache_fox_v[l], page_table), fv], axis=1)
        lf_all = jnp.concatenate([_gather_pages(cache_fox_logf[l], page_table).astype(jnp.float32), lf], axis=1)
        ckv_all = jnp.concatenate([_gather_pages(cache_mla_ckv[l], page_table), ckv], axis=1)
        kr_all = jnp.concatenate([_gather_pages(cache_mla_krope[l], page_table), kr], axis=1)
        fo = _fox_attention(fq, k_all, v_all, lf_all, pos_s, pos_s_keys)
        mo = _mla_attention(ql, qr, ckv_all, kr_all, pos_s, pos_s_keys)
        ys = ys + _merge_heads(fo, mo, *out_w)
        s_k.append(fk); s_v.append(fv); s_f.append(lf); s_c.append(ckv); s_r.append(kr)

        hp = _rmsnorm(yp, g_ffn[l])
        hs = _rmsnorm(ys, g_ffn[l])
        i = l // 2
        if l % 2 == 0:
            yp = yp + _swiglu(hp, w_gate_dense[i], w_up_dense[i], w_down_dense[i])
            ys = ys + _swiglu(hs, w_gate_dense[i], w_up_dense[i], w_down_dense[i])
        else:
            yp = yp + _moe_swiglu(hp, w_router[i], w_gate_exp[i], w_up_exp[i], w_down_exp[i])
            ys = ys + _moe_swiglu(hs, w_router[i], w_gate_exp[i], w_up_exp[i], w_down_exp[i])

    y_prompt = _rmsnorm(yp, g_final)
    y_sample = _rmsnorm(ys, g_final)
    return (y_prompt, y_sample,
            jnp.stack(p_k), jnp.stack(p_v), jnp.stack(p_f), jnp.stack(p_c), jnp.stack(p_r),
            jnp.stack(s_k), jnp.stack(s_v), jnp.stack(s_f), jnp.stack(s_c), jnp.stack(s_r))
```

```python
import functools

import jax
import jax.numpy as jnp
from jax import lax
from jax.experimental import pallas as pl
from jax.experimental.pallas import tpu as pltpu

F32 = jnp.float32
BF16 = jnp.bfloat16

D_MODEL = 1024
HEAD_DIM = 64
FOX_HEADS = 8
FOX_KV_HEADS = 4
FOX_WIDTH = FOX_HEADS * HEAD_DIM
FOX_KV_WIDTH = FOX_KV_HEADS * HEAD_DIM
MLA_HEADS = 8
MLA_NOPE_DIM = 64
MLA_ROPE_DIM = 32
MLA_V_DIM = 64
MLA_Q_RANK = 256
MLA_KV_RANK = 128
MLA_WIDTH = MLA_HEADS * MLA_V_DIM
MLA_QK_DIM = MLA_KV_RANK + MLA_ROPE_DIM
N_EXPERTS = 8
ROPE_THETA = 10000.0
NORM_EPS = 1e-6
FOX_SCALE = HEAD_DIM ** -0.5
MLA_SCALE = (MLA_NOPE_DIM + MLA_ROPE_DIM) ** -0.5
MASKED = -1e30

LANES = 128
VMEM_LIMIT_BYTES = 48 * 1024 * 1024

C_FQ = 0
C_FK = C_FQ + FOX_WIDTH
C_FV = C_FK + FOX_KV_WIDTH
C_CQ = C_FV + FOX_KV_WIDTH
C_CKV = C_CQ + MLA_Q_RANK
C_MISC = C_CKV + MLA_KV_RANK
PROJ_COLS = C_MISC + LANES
M_KRR = MLA_ROPE_DIM
M_FZ = 2 * MLA_ROPE_DIM

TOKEN_TILE = 512
NT_DIMS = (((1,), (1,)), ((), ()))


def _largest_tile(cap, *sizes):
    t = cap
    while any(s % t for s in sizes):
        t //= 2
    return t


FF_TILE_CAP = 1536


def _ff_tile(ff):
    return max(c for c in range(LANES, FF_TILE_CAP + 1, LANES) if ff % c == 0)


def _rms(x, g):
    return x * lax.rsqrt(jnp.mean(x * x, axis=-1, keepdims=True) + NORM_EPS) * g


def _params(*sem):
    return pltpu.CompilerParams(dimension_semantics=sem, vmem_limit_bytes=VMEM_LIMIT_BYTES)


def _proj_kernel(x_ref, g_ref, w_ref, gckv_ref, bf_ref, cos_ref, sin_ref,
                 fq_ref, fk_ref, fv_ref, fkb_ref, fvb_ref, cq_ref, ckv_ref, kr_ref, kcat_ref, lf_ref):
    xn = _rms(x_ref[...], g_ref[...])
    p = jnp.dot(xn.astype(BF16), w_ref[...], preferred_element_type=F32)
    fq_ref[...] = p[:, C_FQ:C_FK].astype(BF16)
    fk = p[:, C_FK:C_FV]
    fv = p[:, C_FV:C_CQ]
    fk_ref[...] = fk
    fv_ref[...] = fv
    fkb_ref[...] = fk.astype(BF16)
    fvb_ref[...] = fv.astype(BF16)
    cq_ref[...] = p[:, C_CQ:C_CKV]
    ckv = _rms(p[:, C_CKV:C_MISC], gckv_ref[...])
    ckv_ref[...] = ckv
    misc = p[:, C_MISC:PROJ_COLS]
    rot = misc * cos_ref[...] + pltpu.roll(misc, LANES - M_KRR, 1) * sin_ref[...]
    kr = rot[:, 0:MLA_ROPE_DIM]
    kr_ref[...] = kr
    kcat_ref[:, 0:MLA_KV_RANK] = ckv.astype(BF16)
    kcat_ref[:, MLA_KV_RANK:MLA_QK_DIM] = kr.astype(BF16)
    z = pltpu.roll(misc, LANES - M_FZ, 1)[:, 0:FOX_HEADS] + bf_ref[...]
    lf_ref[...] = jnp.minimum(z, 0.0) - jnp.log1p(jnp.exp(-jnp.abs(z)))


def _proj(x, g, w, gckv, bf, cos, sin, tm):
    t = x.shape[0]
    row = lambda n: pl.BlockSpec((tm, n), lambda i: (i, 0))
    full = lambda a: pl.BlockSpec(a.shape, lambda i: (0,) * a.ndim)
    outs = [(FOX_WIDTH, BF16), (FOX_KV_WIDTH, F32), (FOX_KV_WIDTH, F32), (FOX_KV_WIDTH, BF16),
            (FOX_KV_WIDTH, BF16), (MLA_Q_RANK, F32), (MLA_KV_RANK, F32), (MLA_ROPE_DIM, F32),
            (MLA_QK_DIM, BF16), (FOX_HEADS, F32)]
    return pl.pallas_call(
        _proj_kernel,
        grid=(t // tm,),
        in_specs=[row(D_MODEL), full(g), full(w), full(gckv), full(bf), row(LANES), row(LANES)],
        out_specs=[row(n) for n, _ in outs],
        out_shape=[jax.ShapeDtypeStruct((t, n), d) for n, d in outs],
        compiler_params=_params("parallel"),
        name="in_proj",
    )(x, g, w, gckv, bf, cos, sin)


def _mlaq_kernel(cq_ref, g_ref, wq_ref, wuk_ref, cos_ref, sin_ref, o_ref):
    cqn = _rms(cq_ref[...], g_ref[...]).astype(BF16)
    for h in range(MLA_HEADS):
        qa = jnp.dot(cqn, wq_ref[h], preferred_element_type=F32)
        lat = jnp.dot(qa[:, 0:LANES].astype(BF16), wuk_ref[h], preferred_element_type=F32)
        rot = qa[:, LANES:2 * LANES] * cos_ref[...] + qa[:, 2 * LANES:3 * LANES] * sin_ref[...]
        o_ref[h, :, 0:MLA_KV_RANK] = (lat * MLA_SCALE).astype(BF16)
        o_ref[h, :, MLA_KV_RANK:MLA_QK_DIM] = (rot[:, 0:MLA_ROPE_DIM] * MLA_SCALE).astype(BF16)


def _mla_q(cq, g, wq, wuk, cos, sin, tm):
    t = cq.shape[0]
    row = lambda n: pl.BlockSpec((tm, n), lambda i: (i, 0))
    full = lambda a: pl.BlockSpec(a.shape, lambda i: (0,) * a.ndim)
    return pl.pallas_call(
        _mlaq_kernel,
        grid=(t // tm,),
        in_specs=[row(MLA_Q_RANK), full(g), full(wq), full(wuk), row(LANES), row(LANES)],
        out_specs=pl.BlockSpec((MLA_HEADS, tm, MLA_QK_DIM), lambda i: (0, i, 0)),
        out_shape=jax.ShapeDtypeStruct((MLA_HEADS, t, MLA_QK_DIM), BF16),
        compiler_params=_params("parallel"),
        name="mla_q",
    )(cq, g, wq, wuk, cos, sin)


def _cumsum_kernel(lf_ref, o_ref, carry_sc):
    @pl.when(pl.program_id(1) == 0)
    def _():
        carry_sc[...] = jnp.zeros_like(carry_sc)

    n = lf_ref.shape[0]
    tri = (lax.broadcasted_iota(jnp.int32, (n, n), 1) <= lax.broadcasted_iota(jnp.int32, (n, n), 0)).astype(F32)
    c = jnp.dot(tri, lf_ref[...], precision=lax.Precision.HIGHEST, preferred_element_type=F32) + carry_sc[...]
    o_ref[...] = c
    carry_sc[...] = c[n - 1:n, :]


def _cumsum(lf, tile):
    b, s, h = lf.shape
    spec = pl.BlockSpec((None, tile, h), lambda i, j: (i, j, 0))
    return pl.pallas_call(
        _cumsum_kernel,
        grid=(b, s // tile),
        in_specs=[spec],
        out_specs=spec,
        out_shape=jax.ShapeDtypeStruct(lf.shape, F32),
        scratch_shapes=[pltpu.VMEM((1, h), F32)],
        compiler_params=_params("parallel", "arbitrary"),
        name="logf_cumsum",
    )(lf)


def _last_key_block(qi, tq, tk):
    return ((qi + 1) * tq - 1) // tk


def _fox_flash_kernel(q_ref, k_ref, v_ref, cq_ref, ck_ref, o_ref, m_sc, l_sc, acc_sc, *, tq, tk):
    qi = pl.program_id(2)
    ki = pl.program_id(3)
    last = _last_key_block(qi, tq, tk)

    @pl.when(ki == 0)
    def _():
        m_sc[...] = jnp.full_like(m_sc, MASKED)
        l_sc[...] = jnp.zeros_like(l_sc)
        acc_sc[...] = jnp.zeros_like(acc_sc)

    @pl.when(ki <= last)
    def _():
        qpos = qi * tq + lax.broadcasted_iota(jnp.int32, (tq, tk), 0)
        kpos = ki * tk + lax.broadcasted_iota(jnp.int32, (tq, tk), 1)
        causal = kpos <= qpos
        for j in range(2):
            kj = k_ref[:, j * HEAD_DIM:(j + 1) * HEAD_DIM]
            vj = v_ref[:, j * HEAD_DIM:(j + 1) * HEAD_DIM]
            for g in range(2):
                h = 2 * j + g
                qh = q_ref[:, h * HEAD_DIM:(h + 1) * HEAD_DIM]
                s = lax.dot_general(qh, kj, NT_DIMS, preferred_element_type=F32)
                u = jnp.where(causal, s - ck_ref[h], MASKED)
                cq = cq_ref[h]
                m_old = m_sc[h]
                m_new = jnp.maximum(m_old, jnp.max(u, axis=-1, keepdims=True) + cq)
                p = jnp.exp(u - (m_new - cq))
                alpha = jnp.exp(m_old - m_new)
                l_sc[h] = alpha * l_sc[h] + jnp.sum(p, axis=-1, keepdims=True)
                acc_sc[h] = alpha * acc_sc[h] + jnp.dot(p.astype(BF16), vj, preferred_element_type=F32)
                m_sc[h] = m_new

    @pl.when(ki == last)
    def _():
        for h in range(4):
            o_ref[:, h * HEAD_DIM:(h + 1) * HEAD_DIM] = acc_sc[h] / l_sc[h]


def _fox_prompt_attention(q, k, v, cq, ck, batch, seq, tq, tk):
    nq, nk = seq // tq, seq // tk
    kmap = lambda b, hb, i, j: (b * nk + jnp.minimum(j, _last_key_block(i, tq, tk)), hb)
    return pl.pallas_call(
        functools.partial(_fox_flash_kernel, tq=tq, tk=tk),
        grid=(batch, 2, nq, nk),
        in_specs=[
            pl.BlockSpec((tq, 4 * HEAD_DIM), lambda b, hb, i, j: (b * nq + i, hb)),
            pl.BlockSpec((tk, 2 * HEAD_DIM), kmap),
            pl.BlockSpec((tk, 2 * HEAD_DIM), kmap),
            pl.BlockSpec((None, 4, tq, 1), lambda b, hb, i, j: (b, hb, i, 0)),
            pl.BlockSpec((None, 4, 1, tk),
                         lambda b, hb, i, j: (b, hb, 0, jnp.minimum(j, _last_key_block(i, tq, tk)))),
        ],
        out_specs=pl.BlockSpec((tq, 4 * HEAD_DIM), lambda b, hb, i, j: (b * nq + i, hb)),
        out_shape=jax.ShapeDtypeStruct((batch * seq, FOX_WIDTH), F32),
        scratch_shapes=[pltpu.VMEM((4, tq, 1), F32), pltpu.VMEM((4, tq, 1), F32),
                        pltpu.VMEM((4, tq, HEAD_DIM), F32)],
        compiler_params=_params("parallel", "parallel", "parallel", "arbitrary"),
        name="fox_prompt_attn",
    )(q, k, v, cq, ck)


def _mla_flash_kernel(q_ref, k_ref, o_ref, m_sc, l_sc, acc_sc, *, tq, tk):
    qi = pl.program_id(1)
    ki = pl.program_id(2)
    last = _last_key_block(qi, tq, tk)
    rows = MLA_HEADS * tq

    @pl.when(ki == 0)
    def _():
        m_sc[...] = jnp.full_like(m_sc, MASKED)
        l_sc[...] = jnp.zeros_like(l_sc)
        acc_sc[...] = jnp.zeros_like(acc_sc)

    @pl.when(ki <= last)
    def _():
        q = q_ref[...].reshape(rows, MLA_QK_DIM)
        kc = k_ref[...]
        s = lax.dot_general(q, kc, NT_DIMS, preferred_element_type=F32)
        qpos = qi * tq + lax.rem(lax.broadcasted_iota(jnp.int32, (rows, tk), 0), tq)
        kpos = ki * tk + lax.broadcasted_iota(jnp.int32, (rows, tk), 1)
        s = jnp.where(kpos <= qpos, s, MASKED)
        m_old = m_sc[...]
        m_new = jnp.maximum(m_old, jnp.max(s, axis=-1, keepdims=True))
        p = jnp.exp(s - m_new)
        alpha = jnp.exp(m_old - m_new)
        l_sc[...] = alpha * l_sc[...] + jnp.sum(p, axis=-1, keepdims=True)
        acc_sc[...] = alpha * acc_sc[...] + jnp.dot(p.astype(BF16), kc[:, 0:MLA_KV_RANK],
                                                    preferred_element_type=F32)
        m_sc[...] = m_new

    @pl.when(ki == last)
    def _():
        for h in range(MLA_HEADS):
            o_ref[:, h * MLA_KV_RANK:(h + 1) * MLA_KV_RANK] = (
                acc_sc[h * tq:(h + 1) * tq, :] / l_sc[h * tq:(h + 1) * tq, :])


def _mla_prompt_attention(q, kcat, batch, seq, tq, tk):
    nq, nk = seq // tq, seq // tk
    rows = MLA_HEADS * tq
    return pl.pallas_call(
        functools.partial(_mla_flash_kernel, tq=tq, tk=tk),
        grid=(batch, nq, nk),
        in_specs=[
            pl.BlockSpec((MLA_HEADS, tq, MLA_QK_DIM), lambda b, i, j: (0, b * nq + i, 0)),
            pl.BlockSpec((tk, MLA_QK_DIM),
                         lambda b, i, j: (b * nk + jnp.minimum(j, _last_key_block(i, tq, tk)), 0)),
        ],
        out_specs=pl.BlockSpec((tq, MLA_HEADS * MLA_KV_RANK), lambda b, i, j: (b * nq + i, 0)),
        out_shape=jax.ShapeDtypeStruct((batch * seq, MLA_HEADS * MLA_KV_RANK), F32),
        scratch_shapes=[pltpu.VMEM((rows, 1), F32), pltpu.VMEM((rows, 1), F32),
                        pltpu.VMEM((rows, MLA_KV_RANK), F32)],
        compiler_params=_params("parallel", "parallel", "arbitrary"),
        name="mla_prompt_attn",
    )(q, kcat)


SAMPLE_ROWS = 4 * FOX_HEADS
PAGES_PER_STEP = 8


def _sample_attn_kernel(pt_ref, qf_ref, qm_ref, kn_ref, vn_ref, cn_ref, rn_ref, lfn_ref,
                        ck_hbm, cv_hbm, cc_hbm, cr_hbm, cl_hbm,
                        of_ref, om_ref,
                        kbuf, vbuf, cbuf, rbuf, lbuf, sems, m_sc, l_sc, accf_sc, accm_sc, run_sc, cnew_sc,
                        *, layer, n_pages, n_seq, page, dec_seq):
    b = pl.program_id(0)
    step = pl.program_id(1)
    pp = PAGES_PER_STEP
    nch = n_pages // pp
    nr = SAMPLE_ROWS
    hi = lax.Precision.HIGHEST

    def chunk_copies(bb, cc, slot):
        out = []
        for j in range(pp):
            pg = pt_ref[bb * n_pages + n_pages - (cc + 1) * pp + j]
            for a, (hbm, buf) in enumerate(((ck_hbm, kbuf), (cv_hbm, vbuf), (cc_hbm, cbuf),
                                            (cr_hbm, rbuf), (cl_hbm, lbuf))):
                out.append(pltpu.make_async_copy(hbm.at[layer, pg], buf.at[slot, j], sems.at[a, slot]))
        return out

    @pl.when((b == 0) & (step == 0))
    def _():
        for cp in chunk_copies(0, 0, 0):
            cp.start()

    row_i = lax.broadcasted_iota(jnp.int32, (nr, FOX_HEADS), 0)
    col_i = lax.broadcasted_iota(jnp.int32, (nr, FOX_HEADS), 1)
    expand = (row_i // dec_seq == col_i).astype(F32)
    r_i = lax.broadcasted_iota(jnp.int32, (page, page), 0)
    c_i = lax.broadcasted_iota(jnp.int32, (page, page), 1)

    def attend(s_f, s_m, v_f, v_m, first):
        s = jnp.concatenate([s_f, s_m], axis=0)
        if first:
            m_new = jnp.max(s, axis=-1, keepdims=True)
            p = jnp.exp(s - m_new)
            l_sc[...] = jnp.sum(p, axis=-1, keepdims=True)
            accf_sc[...] = jnp.dot(p[0:nr].astype(BF16), v_f, preferred_element_type=F32)
            accm_sc[...] = jnp.dot(p[nr:2 * nr].astype(BF16), v_m, preferred_element_type=F32)
        else:
            m_old = m_sc[...]
            m_new = jnp.maximum(m_old, jnp.max(s, axis=-1, keepdims=True))
            p = jnp.exp(s - m_new)
            alpha = jnp.exp(m_old - m_new)
            l_sc[...] = alpha * l_sc[...] + jnp.sum(p, axis=-1, keepdims=True)
            accf_sc[...] = alpha[0:nr] * accf_sc[...] + jnp.dot(p[0:nr].astype(BF16), v_f,
                                                                preferred_element_type=F32)
            accm_sc[...] = alpha[nr:2 * nr] * accm_sc[...] + jnp.dot(p[nr:2 * nr].astype(BF16), v_m,
                                                                     preferred_element_type=F32)
        m_sc[...] = m_new

    def scores(k, c, r):
        qm = qm_ref[...]
        s_f = lax.dot_general(qf_ref[...], k, NT_DIMS, preferred_element_type=F32)
        s_m = (lax.dot_general(qm[:, 0:MLA_KV_RANK], c, NT_DIMS, preferred_element_type=F32)
               + lax.dot_general(qm[:, MLA_KV_RANK:MLA_QK_DIM], r, NT_DIMS, preferred_element_type=F32))
        return s_f, s_m

    @pl.when(step == 0)
    def _():
        lf_t = lax.dot_general(expand, lfn_ref[...], NT_DIMS, precision=hi, preferred_element_type=F32)
        incl = (r_i <= c_i).astype(F32)
        c_key = jnp.dot(lf_t, incl, precision=hi, preferred_element_type=F32)
        tok = lax.rem(lax.broadcasted_iota(jnp.int32, (nr, page), 0), dec_seq)
        lane = lax.broadcasted_iota(jnp.int32, (nr, page), 1)
        c_new = jnp.sum(jnp.where(lane == tok, c_key, 0.0), axis=-1, keepdims=True)
        visible = lane <= tok
        k = kn_ref[...].astype(BF16)
        v = vn_ref[...].astype(BF16)
        c = cn_ref[...].astype(BF16)
        r = rn_ref[...].astype(BF16)
        s_f, s_m = scores(k, c, r)
        s_f = jnp.where(visible, s_f + (c_new - c_key), MASKED)
        s_m = jnp.where(visible, s_m, MASKED)
        attend(s_f, s_m, v, c, True)
        run_sc[...] = jnp.zeros_like(run_sc)
        cnew_sc[...] = c_new

    @pl.when(step >= 1)
    def _():
        cc = step - 1
        n = b * nch + cc
        slot = lax.rem(n, 2)
        for cp in chunk_copies(b, cc, slot):
            cp.wait()

        @pl.when(n + 1 < n_seq * nch)
        def _():
            wrap = cc + 1 == nch
            nb = jnp.where(wrap, b + 1, b)
            nc = jnp.where(wrap, 0, cc + 1)
            for cp in chunk_copies(nb, nc, 1 - slot):
                cp.start()

        strict = (r_i > c_i).astype(F32)
        lf_t = [lax.dot_general(expand, lbuf[slot, j], NT_DIMS, precision=hi, preferred_element_type=F32)
                for j in range(pp)]
        within = jnp.dot(jnp.concatenate(lf_t, axis=0), strict, precision=hi, preferred_element_type=F32)
        later = run_sc[...] + cnew_sc[...]
        bias = [None] * pp
        for j in reversed(range(pp)):
            bias[j] = within[j * nr:(j + 1) * nr, :] + later
            later = later + jnp.sum(lf_t[j], axis=-1, keepdims=True)
        run_sc[...] = later - cnew_sc[...]
        k = kbuf[slot].reshape(pp * page, FOX_KV_WIDTH).astype(BF16)
        v = vbuf[slot].reshape(pp * page, FOX_KV_WIDTH).astype(BF16)
        c = cbuf[slot].reshape(pp * page, MLA_KV_RANK).astype(BF16)
        r = rbuf[slot].reshape(pp * page, MLA_ROPE_DIM).astype(BF16)
        s_f, s_m = scores(k, c, r)
        attend(s_f + jnp.concatenate(bias, axis=1), s_m, v, c, False)

    @pl.when(step == nch)
    def _():
        l = l_sc[...]
        of_ref[...] = accf_sc[...] / l[0:nr]
        om_ref[...] = accm_sc[...] / l[nr:2 * nr]


def _sample_attention(layer, page_table, qf, qm, kn, vn, cn, rn, lfn, ck, cv, cc, cr, cl):
    n_seq, n_pages = page_table.shape
    page = ck.shape[2]
    dec_seq = SAMPLE_ROWS // FOX_HEADS
    pp = PAGES_PER_STEP
    nch = n_pages // pp
    per_seq = lambda *tail: pl.BlockSpec((None,) + tail, lambda b, s, pt: (b,) + (0,) * len(tail))
    hbm = pl.BlockSpec(memory_space=pl.ANY)
    kern = functools.partial(_sample_attn_kernel, layer=layer, n_pages=n_pages, n_seq=n_seq, page=page,
                             dec_seq=dec_seq)
    return pl.pallas_call(
        kern,
        grid_spec=pltpu.PrefetchScalarGridSpec(
            num_scalar_prefetch=1,
            grid=(n_seq, nch + 1),
            in_specs=[per_seq(SAMPLE_ROWS, FOX_KV_WIDTH), per_seq(SAMPLE_ROWS, MLA_QK_DIM),
                      per_seq(page, FOX_KV_WIDTH), per_seq(page, FOX_KV_WIDTH), per_seq(page, MLA_KV_RANK),
                      per_seq(page, MLA_ROPE_DIM), per_seq(page, FOX_HEADS),
                      hbm, hbm, hbm, hbm, hbm],
            out_specs=[per_seq(SAMPLE_ROWS, FOX_KV_WIDTH), per_seq(SAMPLE_ROWS, MLA_KV_RANK)],
            scratch_shapes=[
                pltpu.VMEM((2, pp, page, FOX_KV_WIDTH), F32), pltpu.VMEM((2, pp, page, FOX_KV_WIDTH), F32),
                pltpu.VMEM((2, pp, page, MLA_KV_RANK), F32), pltpu.VMEM((2, pp, page, MLA_ROPE_DIM), F32),
                pltpu.VMEM((2, pp, page, FOX_HEADS), F32), pltpu.SemaphoreType.DMA((5, 2)),
                pltpu.VMEM((2 * SAMPLE_ROWS, 1), F32), pltpu.VMEM((2 * SAMPLE_ROWS, 1), F32),
                pltpu.VMEM((SAMPLE_ROWS, FOX_KV_WIDTH), F32), pltpu.VMEM((SAMPLE_ROWS, MLA_KV_RANK), F32),
                pltpu.VMEM((SAMPLE_ROWS, 1), F32), pltpu.VMEM((SAMPLE_ROWS, 1), F32)],
        ),
        out_shape=[jax.ShapeDtypeStruct((n_seq, SAMPLE_ROWS, FOX_KV_WIDTH), F32),
                   jax.ShapeDtypeStruct((n_seq, SAMPLE_ROWS, MLA_KV_RANK), F32)],
        compiler_params=_params("arbitrary", "arbitrary"),
        name="sample_attn",
    )(page_table.reshape(-1), qf, qm, kn, vn, cn, rn, lfn, ck, cv, cc, cr, cl)


def _merge_kernel(fo_ref, lat_ref, x_ref, gf_ref, gm_ref, wuv_ref, wo_ref, o_ref):
    fn = _rms(fo_ref[...], gf_ref[...])
    mo = jnp.dot(lat_ref[...].astype(BF16), wuv_ref[...], preferred_element_type=F32)
    mn = _rms(mo, gm_ref[...])
    y = (jnp.dot(fn.astype(BF16), wo_ref[0:FOX_WIDTH, :], preferred_element_type=F32)
         + jnp.dot(mn.astype(BF16), wo_ref[FOX_WIDTH:FOX_WIDTH + MLA_WIDTH, :], preferred_element_type=F32))
    o_ref[...] = x_ref[...] + y


def _merge(fo, lat, x, gf, gm, wuv, wo, tm):
    t = x.shape[0]
    row = lambda n: pl.BlockSpec((tm, n), lambda i: (i, 0))
    full = lambda a: pl.BlockSpec(a.shape, lambda i: (0,) * a.ndim)
    return pl.pallas_call(
        _merge_kernel,
        grid=(t // tm,),
        in_specs=[row(FOX_WIDTH), row(MLA_HEADS * MLA_KV_RANK), row(D_MODEL), full(gf), full(gm),
                  full(wuv), full(wo)],
        out_specs=row(D_MODEL),
        out_shape=jax.ShapeDtypeStruct((t, D_MODEL), F32),
        compiler_params=_params("parallel"),
        name="head_merge",
    )(fo, lat, x, gf, gm, wuv, wo)


def _swiglu_chunk(h, wg, wu, wd):
    gate = jnp.dot(h, wg, preferred_element_type=F32)
    up = jnp.dot(h, wu, preferred_element_type=F32)
    a = gate * (1.0 / (1.0 + jnp.exp(-gate))) * up
    return jnp.dot(a.astype(BF16), wd, preferred_element_type=F32)


def _ffn_kernel(x_ref, g_ref, wg_ref, wu_ref, wd_ref, o_ref, h_sc, acc_sc):
    f = pl.program_id(1)

    @pl.when(f == 0)
    def _():
        h_sc[...] = _rms(x_ref[...], g_ref[...]).astype(BF16)
        acc_sc[...] = jnp.zeros_like(acc_sc)

    acc_sc[...] += _swiglu_chunk(h_sc[...], wg_ref[...], wu_ref[...], wd_ref[...])

    @pl.when(f == pl.num_programs(1) - 1)
    def _():
        o_ref[...] = x_ref[...] + acc_sc[...]


def _ffn_dense(x, g, wg, wu, wd, tf, tm):
    t = x.shape[0]
    nf = wg.shape[1] // tf
    return pl.pallas_call(
        _ffn_kernel,
        grid=(t // tm, nf),
        in_specs=[pl.BlockSpec((tm, D_MODEL), lambda i, f: (i, 0)),
                  pl.BlockSpec((1, D_MODEL), lambda i, f: (0, 0)),
                  pl.BlockSpec((D_MODEL, tf), lambda i, f: (0, f)),
                  pl.BlockSpec((D_MODEL, tf), lambda i, f: (0, f)),
                  pl.BlockSpec((tf, D_MODEL), lambda i, f: (f, 0))],
        out_specs=pl.BlockSpec((tm, D_MODEL), lambda i, f: (i, 0)),
        out_shape=jax.ShapeDtypeStruct((t, D_MODEL), F32),
        scratch_shapes=[pltpu.VMEM((tm, D_MODEL), BF16), pltpu.VMEM((tm, D_MODEL), F32)],
        compiler_params=_params("parallel", "arbitrary"),
        name="ffn_dense",
    )(x, g, wg, wu, wd)


def _router_kernel(x_ref, g_ref, wr_ref, gates_ref):
    h = _rms(x_ref[...], g_ref[...])
    logits = jnp.dot(h, wr_ref[...], precision=lax.Precision.HIGHEST, preferred_element_type=F32)
    lane = lax.broadcasted_iota(jnp.int32, logits.shape, 1)
    v1 = jnp.max(logits, axis=-1, keepdims=True)
    i1 = jnp.min(jnp.where(logits == v1, lane, N_EXPERTS), axis=-1, keepdims=True)
    rest = jnp.where(lane == i1, -jnp.inf, logits)
    v2 = jnp.max(rest, axis=-1, keepdims=True)
    i2 = jnp.min(jnp.where(rest == v2, lane, N_EXPERTS), axis=-1, keepdims=True)
    e2 = jnp.exp(v2 - v1)
    gates_ref[...] = jnp.where(lane == i1, 1.0 / (1.0 + e2), 0.0) + jnp.where(lane == i2, e2 / (1.0 + e2), 0.0)


def _router(x, g, wr, tm):
    t = x.shape[0]
    return pl.pallas_call(
        _router_kernel,
        grid=(t // tm,),
        in_specs=[pl.BlockSpec((tm, D_MODEL), lambda i: (i, 0)),
                  pl.BlockSpec((1, D_MODEL), lambda i: (0, 0)),
                  pl.BlockSpec((D_MODEL, N_EXPERTS), lambda i: (0, 0))],
        out_specs=pl.BlockSpec((tm, N_EXPERTS), lambda i: (i, 0)),
        out_shape=jax.ShapeDtypeStruct((t, N_EXPERTS), F32),
        compiler_params=_params("parallel"),
        name="moe_router",
    )(x, g, wr)


def _moe_kernel(x_ref, g_ref, gate_ref, wg_ref, wu_ref, wd_ref, o_ref, h_sc, acc_sc):
    e = pl.program_id(1)
    f = pl.program_id(2)

    @pl.when((e == 0) & (f == 0))
    def _():
        h_sc[...] = _rms(x_ref[...], g_ref[...]).astype(BF16)
        acc_sc[...] = jnp.zeros_like(acc_sc)

    acc_sc[...] += gate_ref[...] * _swiglu_chunk(h_sc[...], wg_ref[...], wu_ref[...], wd_ref[...])

    @pl.when((e == pl.num_programs(1) - 1) & (f == pl.num_programs(2) - 1))
    def _():
        o_ref[...] = x_ref[...] + acc_sc[...]


def _moe_dense(x, g, gates_t, wg, wu, wd, tf, tm):
    t = x.shape[0]
    ne, _, ff = wg.shape
    return pl.pallas_call(
        _moe_kernel,
        grid=(t // tm, ne, ff // tf),
        in_specs=[pl.BlockSpec((tm, D_MODEL), lambda i, e, f: (i, 0)),
                  pl.BlockSpec((1, D_MODEL), lambda i, e, f: (0, 0)),
                  pl.BlockSpec((None, tm, 1), lambda i, e, f: (e, i, 0)),
                  pl.BlockSpec((None, D_MODEL, tf), lambda i, e, f: (e, 0, f)),
                  pl.BlockSpec((None, D_MODEL, tf), lambda i, e, f: (e, 0, f)),
                  pl.BlockSpec((None, tf, D_MODEL), lambda i, e, f: (e, f, 0))],
        out_specs=pl.BlockSpec((tm, D_MODEL), lambda i, e, f: (i, 0)),
        out_shape=jax.ShapeDtypeStruct((t, D_MODEL), F32),
        scratch_shapes=[pltpu.VMEM((tm, D_MODEL), BF16), pltpu.VMEM((tm, D_MODEL), F32)],
        compiler_params=_params("parallel", "arbitrary", "arbitrary"),
        name="moe_experts",
    )(x, g, gates_t, wg, wu, wd)


def _final_norm_kernel(x_ref, g_ref, o_ref):
    o_ref[...] = _rms(x_ref[...], g_ref[...])


def _final_norm(x, g, tm):
    t = x.shape[0]
    return pl.pallas_call(
        _final_norm_kernel,
        grid=(t // tm,),
        in_specs=[pl.BlockSpec((tm, D_MODEL), lambda i: (i, 0)), pl.BlockSpec((1, D_MODEL), lambda i: (0, 0))],
        out_specs=pl.BlockSpec((tm, D_MODEL), lambda i: (i, 0)),
        out_shape=jax.ShapeDtypeStruct((t, D_MODEL), F32),
        compiler_params=_params("parallel"),
        name="final_norm",
    )(x, g)


def _rotate_half_cols(w):
    half = w.shape[-1] // 2
    return jnp.concatenate([-w[..., half:], w[..., :half]], axis=-1)


def _prep_w_in(w):
    ends = [FOX_WIDTH, FOX_WIDTH + FOX_KV_WIDTH, FOX_WIDTH + 2 * FOX_KV_WIDTH]
    ends.append(ends[-1] + FOX_HEADS)
    ends.append(ends[-1] + MLA_Q_RANK)
    ends.append(ends[-1] + MLA_KV_RANK)
    fq, fk, fv, fz, cq, ckv, kr = jnp.split(w, ends, axis=-1)
    pad = jnp.zeros((w.shape[0], LANES - 2 * MLA_ROPE_DIM - FOX_HEADS), w.dtype)
    return jnp.concatenate([fq * FOX_SCALE, fk, fv, cq, ckv, kr, _rotate_half_cols(kr), fz, pad],
                           axis=-1).astype(BF16)


def _prep_w_uq(w_uq):
    nope = w_uq[:, :, :MLA_NOPE_DIM]
    rope = w_uq[:, :, MLA_NOPE_DIM:]
    padto = lambda a: jnp.pad(a, ((0, 0), (0, 0), (0, LANES - a.shape[-1])))
    w = jnp.concatenate([padto(nope), padto(rope), padto(_rotate_half_cols(rope))], axis=-1)
    return jnp.transpose(w, (1, 0, 2)).astype(BF16)


def _prep_w_uk(w_uk):
    w = jnp.transpose(w_uk, (1, 2, 0))
    return jnp.pad(w, ((0, 0), (0, LANES - w.shape[1]), (0, 0))).astype(BF16)


def _prep_w_uv(w_uv):
    c, h, v = w_uv.shape
    bd = jnp.einsum('chv,hg->hcgv', w_uv, jnp.eye(h, dtype=w_uv.dtype))
    return bd.reshape(h * c, h * v).astype(BF16)


def _rope_tables(pos):
    half = MLA_ROPE_DIM // 2
    inv_freq = jnp.power(ROPE_THETA, -jnp.arange(half, dtype=F32) / half)
    ang = pos.astype(F32)[:, None] * inv_freq[None, :]
    pad = lambda a: jnp.pad(jnp.concatenate([a, a], axis=-1), ((0, 0), (0, LANES - MLA_ROPE_DIM)))
    return pad(jnp.cos(ang)), pad(jnp.sin(ang))


def kernel(x_prompt, x_sample, cache_fox_k, cache_fox_v, cache_fox_logf, cache_mla_ckv, cache_mla_krope, page_table, w_in, b_f, g_attn, g_cq, g_ckv, w_uq, w_uk, w_uv, g_fox_out, g_mla_out, w_o, g_ffn, w_gate_dense, w_up_dense, w_down_dense, w_router, w_gate_exp, w_up_exp, w_down_exp, g_final):
    batch, seq, _ = x_prompt.shape
    n_seq, dec_seq, _ = x_sample.shape
    depth, n_pool, page = cache_fox_k.shape[:3]
    n_pages = page_table.shape[1]
    past_len = n_pages * page
    tp = batch * seq
    ts = n_seq * dec_seq
    tm = _largest_tile(TOKEN_TILE, tp, ts)
    tk = _largest_tile(512, seq)
    assert dec_seq * FOX_HEADS == SAMPLE_ROWS and n_pages % PAGES_PER_STEP == 0 and dec_seq <= page

    x = jnp.concatenate([x_prompt.reshape(tp, D_MODEL), x_sample.reshape(ts, D_MODEL)], axis=0)
    pos = jnp.concatenate([jnp.tile(jnp.arange(seq, dtype=jnp.int32), batch),
                           past_len + jnp.tile(jnp.arange(dec_seq, dtype=jnp.int32), n_seq)])
    cos, sin = _rope_tables(pos)

    ck = cache_fox_k.reshape(depth, n_pool, page, FOX_KV_WIDTH)
    cv = cache_fox_v.reshape(depth, n_pool, page, FOX_KV_WIDTH)

    row = lambda a: a.reshape(1, -1)
    outs = [[] for _ in range(10)]
    eye_kv = jnp.eye(FOX_KV_HEADS, dtype=BF16)
    pad_rows = lambda a: jnp.pad(a.reshape(n_seq, dec_seq, -1), ((0, 0), (0, page - dec_seq), (0, 0)))

    for l in range(depth):
        fq, fk, fv, fkb, fvb, cq, ckv, kr, kcat, lf = _proj(
            x, row(g_attn[l]), _prep_w_in(w_in[l]), row(g_ckv[l]), row(b_f[l]), cos, sin, tm)
        qcat = _mla_q(cq, row(g_cq[l]), _prep_w_uq(w_uq[l]), _prep_w_uk(w_uk[l]), cos, sin, tm)

        c = _cumsum(lf[:tp].reshape(batch, seq, FOX_HEADS), tk)
        c_t = jnp.transpose(c, (0, 2, 1))
        fo_p = _fox_prompt_attention(fq, fkb, fvb, c_t[:, :, :, None], c_t[:, :, None, :], batch, seq,
                                     _largest_tile(256, seq), tk)
        lat_p = _mla_prompt_attention(qcat, kcat, batch, seq, _largest_tile(128, seq), tk)

        q5 = fq[tp:].reshape(n_seq, dec_seq, FOX_KV_HEADS, 2, HEAD_DIM)
        qf = jnp.einsum('bqkgd,kj->bkgqjd', q5, eye_kv).reshape(n_seq, SAMPLE_ROWS, FOX_KV_WIDTH)
        qm = jnp.transpose(qcat[:, tp:].reshape(MLA_HEADS, n_seq, dec_seq, MLA_QK_DIM),
                           (1, 0, 2, 3)).reshape(n_seq, SAMPLE_ROWS, MLA_QK_DIM)
        of, om = _sample_attention(l, page_table, qf, qm, pad_rows(fk[tp:]), pad_rows(fv[tp:]),
                                   pad_rows(ckv[tp:]), pad_rows(kr[tp:]), pad_rows(lf[tp:]),
                                   ck, cv, cache_mla_ckv, cache_mla_krope, cache_fox_logf)
        of6 = of.reshape(n_seq, FOX_KV_HEADS, 2, dec_seq, FOX_KV_HEADS, HEAD_DIM)
        fo_s = jnp.einsum('bkgqjd,kj->bqkgd', of6, jnp.eye(FOX_KV_HEADS, dtype=F32)).reshape(ts, FOX_WIDTH)
        lat_s = jnp.transpose(om.reshape(n_seq, MLA_HEADS, dec_seq, MLA_KV_RANK),
                              (0, 2, 1, 3)).reshape(ts, MLA_HEADS * MLA_KV_RANK)

        x = _merge(jnp.concatenate([fo_p, fo_s], axis=0), jnp.concatenate([lat_p, lat_s], axis=0), x,
                   row(g_fox_out[l]), row(g_mla_out[l]), _prep_w_uv(w_uv[l]), w_o[l].astype(BF16), tm)

        i = l // 2
        if l % 2 == 0:
            x = _ffn_dense(x, row(g_ffn[l]), w_gate_dense[i].astype(BF16), w_up_dense[i].astype(BF16),
                           w_down_dense[i].astype(BF16), _ff_tile(w_gate_dense.shape[-1]), tm)
        else:
            gates = _router(x, row(g_ffn[l]), w_router[i], tm)
            x = _moe_dense(x, row(g_ffn[l]), jnp.transpose(gates)[:, :, None], w_gate_exp[i].astype(BF16),
                           w_up_exp[i].astype(BF16), w_down_exp[i].astype(BF16),
                           _ff_tile(w_gate_exp.shape[-1]), tm)

        for dst, a in zip(outs, (fk[:tp], fv[:tp], lf[:tp], ckv[:tp], kr[:tp],
                                 fk[tp:], fv[tp:], lf[tp:], ckv[tp:], kr[tp:])):
            dst.append(a)

    y = _final_norm(x, row(g_final), tm)
    shapes = [(batch, seq, FOX_KV_HEADS, HEAD_DIM), (batch, seq, FOX_KV_HEADS, HEAD_DIM), (batch, seq, FOX_HEADS),
              (batch, seq, MLA_KV_RANK), (batch, seq, MLA_ROPE_DIM),
              (n_seq, dec_seq, FOX_KV_HEADS, HEAD_DIM), (n_seq, dec_seq, FOX_KV_HEADS, HEAD_DIM),
              (n_seq, dec_seq, FOX_HEADS), (n_seq, dec_seq, MLA_KV_RANK), (n_seq, dec_seq, MLA_ROPE_DIM)]
    caches = tuple(jnp.stack(o).reshape((depth,) + s) for o, s in zip(outs, shapes))
    return (y[:tp].reshape(batch, seq, D_MODEL), y[tp:].reshape(n_seq, dec_seq, D_MODEL)) + caches
```

```python
import functools
import math

import jax
import jax.numpy as jnp
import numpy as np
from jax import lax
from jax.experimental import pallas as pl
from jax.experimental.pallas import tpu as pltpu

F32 = jnp.float32
BF16 = jnp.bfloat16
I32 = jnp.int32

D_MODEL = 1024
HEAD_DIM = 64
FOX_HEADS = 8
FOX_KV_HEADS = 4
FOX_GROUP = FOX_HEADS // FOX_KV_HEADS
FOX_WIDTH = FOX_HEADS * HEAD_DIM
FOX_KV_WIDTH = FOX_KV_HEADS * HEAD_DIM
MLA_HEADS = 8
MLA_NOPE_DIM = 64
MLA_ROPE_DIM = 32
MLA_V_DIM = 64
MLA_Q_RANK = 256
MLA_KV_RANK = 128
MLA_WIDTH = MLA_HEADS * MLA_V_DIM
MLA_QK_DIM = MLA_KV_RANK + MLA_ROPE_DIM
N_EXPERTS = 8
TOP_K = 2
ROPE_THETA = 10000.0
NORM_EPS = 1e-6
LOG2E = math.log2(math.e)
FOX_SCALE = HEAD_DIM ** -0.5
MLA_SCALE = (MLA_NOPE_DIM + MLA_ROPE_DIM) ** -0.5
MASKED = -1e30

LANES = 128
SUBLANES = 8
VMEM_LIMIT_BYTES = 48 * 1024 * 1024

C_FQ = 0
C_FK = C_FQ + FOX_WIDTH
C_FV = C_FK + FOX_KV_WIDTH
C_CQ = C_FV + FOX_KV_WIDTH
C_CKV = C_CQ + MLA_Q_RANK
C_MISC = C_CKV + MLA_KV_RANK
PROJ_COLS = C_MISC + LANES
M_KRR = MLA_ROPE_DIM
M_FZ = 2 * MLA_ROPE_DIM

TOKEN_TILE = 512
NT_DIMS = (((1,), (1,)), ((), ()))


def _largest_tile(cap, *sizes):
    t = cap
    while any(s % t for s in sizes):
        t //= 2
    return t


FF_TILE_CAP = 1536


def _ff_tile(ff):
    return max(c for c in range(LANES, FF_TILE_CAP + 1, LANES) if ff % c == 0)


def _rms(x, g):
    return x * lax.rsqrt(jnp.mean(x * x, axis=-1, keepdims=True) + NORM_EPS) * g


def _params(*sem):
    return pltpu.CompilerParams(dimension_semantics=sem, vmem_limit_bytes=VMEM_LIMIT_BYTES)


def _proj_kernel(x_ref, g_ref, w_ref, gckv_ref, bf_ref, cos_ref, sin_ref,
                 fq_ref, fk_ref, fv_ref, fkb_ref, fvb_ref, cq_ref, ckv_ref, kr_ref, kcat_ref, lf_ref):
    xn = _rms(x_ref[...], g_ref[...])
    p = jnp.dot(xn.astype(BF16), w_ref[...], preferred_element_type=F32)
    fq_ref[...] = p[:, C_FQ:C_FK].astype(BF16)
    fk = p[:, C_FK:C_FV]
    fv = p[:, C_FV:C_CQ]
    fk_ref[...] = fk
    fv_ref[...] = fv
    fkb_ref[...] = fk.astype(BF16)
    fvb_ref[...] = fv.astype(BF16)
    cq_ref[...] = p[:, C_CQ:C_CKV]
    ckv = _rms(p[:, C_CKV:C_MISC], gckv_ref[...])
    ckv_ref[...] = ckv
    misc = p[:, C_MISC:PROJ_COLS]
    rot = misc * cos_ref[...] + pltpu.roll(misc, LANES - M_KRR, 1) * sin_ref[...]
    kr = rot[:, 0:MLA_ROPE_DIM]
    kr_ref[...] = kr
    kcat_ref[:, 0:MLA_KV_RANK] = ckv.astype(BF16)
    kcat_ref[:, MLA_KV_RANK:MLA_QK_DIM] = kr.astype(BF16)
    z = pltpu.roll(misc, LANES - M_FZ, 1)[:, 0:FOX_HEADS] + bf_ref[...]
    lf_ref[...] = jnp.minimum(z, 0.0) - jnp.log1p(jnp.exp(-jnp.abs(z)))


def _proj(x, g, w, gckv, bf, cos, sin, tm):
    t = x.shape[0]
    row = lambda n: pl.BlockSpec((tm, n), lambda i: (i, 0))
    full = lambda a: pl.BlockSpec(a.shape, lambda i: (0,) * a.ndim)
    outs = [(FOX_WIDTH, BF16), (FOX_KV_WIDTH, F32), (FOX_KV_WIDTH, F32), (FOX_KV_WIDTH, BF16),
            (FOX_KV_WIDTH, BF16), (MLA_Q_RANK, F32), (MLA_KV_RANK, F32), (MLA_ROPE_DIM, F32),
            (MLA_QK_DIM, BF16), (FOX_HEADS, F32)]
    return pl.pallas_call(
        _proj_kernel,
        grid=(t // tm,),
        in_specs=[row(D_MODEL), full(g), full(w), full(gckv), full(bf), row(LANES), row(LANES)],
        out_specs=[row(n) for n, _ in outs],
        out_shape=[jax.ShapeDtypeStruct((t, n), d) for n, d in outs],
        compiler_params=_params("parallel"),
        name="in_proj",
    )(x, g, w, gckv, bf, cos, sin)


def _mlaq_kernel(cq_ref, g_ref, wq_ref, wuk_ref, cos_ref, sin_ref, o_ref):
    cqn = _rms(cq_ref[...], g_ref[...]).astype(BF16)
    scale = MLA_SCALE * LOG2E
    for h in range(MLA_HEADS):
        qa = jnp.dot(cqn, wq_ref[h], preferred_element_type=F32)
        lat = jnp.dot(qa[:, 0:LANES].astype(BF16), wuk_ref[h], preferred_element_type=F32)
        rot = qa[:, LANES:2 * LANES] * cos_ref[...] + qa[:, 2 * LANES:3 * LANES] * sin_ref[...]
        o_ref[h, :, 0:MLA_KV_RANK] = (lat * scale).astype(BF16)
        o_ref[h, :, MLA_KV_RANK:MLA_QK_DIM] = (rot[:, 0:MLA_ROPE_DIM] * scale).astype(BF16)


def _mla_q(cq, g, wq, wuk, cos, sin, tm):
    t = cq.shape[0]
    row = lambda n: pl.BlockSpec((tm, n), lambda i: (i, 0))
    full = lambda a: pl.BlockSpec(a.shape, lambda i: (0,) * a.ndim)
    return pl.pallas_call(
        _mlaq_kernel,
        grid=(t // tm,),
        in_specs=[row(MLA_Q_RANK), full(g), full(wq), full(wuk), row(LANES), row(LANES)],
        out_specs=pl.BlockSpec((MLA_HEADS, tm, MLA_QK_DIM), lambda i: (0, i, 0)),
        out_shape=jax.ShapeDtypeStruct((MLA_HEADS, t, MLA_QK_DIM), BF16),
        compiler_params=_params("parallel"),
        name="mla_q",
    )(cq, g, wq, wuk, cos, sin)


def _cumsum_kernel(lf_ref, o_ref, carry_sc):
    @pl.when(pl.program_id(1) == 0)
    def _():
        carry_sc[...] = jnp.zeros_like(carry_sc)

    n = lf_ref.shape[0]
    tri = (lax.broadcasted_iota(I32, (n, n), 1) <= lax.broadcasted_iota(I32, (n, n), 0)).astype(F32)
    c = jnp.dot(tri, lf_ref[...], precision=lax.Precision.HIGHEST, preferred_element_type=F32) + carry_sc[...]
    o_ref[...] = c * LOG2E
    carry_sc[...] = c[n - 1:n, :]


def _cumsum(lf, tile):
    b, s, h = lf.shape
    spec = pl.BlockSpec((None, tile, h), lambda i, j: (i, j, 0))
    return pl.pallas_call(
        _cumsum_kernel,
        grid=(b, s // tile),
        in_specs=[spec],
        out_specs=spec,
        out_shape=jax.ShapeDtypeStruct(lf.shape, F32),
        scratch_shapes=[pltpu.VMEM((1, h), F32)],
        compiler_params=_params("parallel", "arbitrary"),
        name="logf_cumsum",
    )(lf)


def _last_key_block(qi, tq, tk):
    return ((qi + 1) * tq - 1) // tk


def _triangle(nq, tq, tk):
    pairs = [(i, j) for i in range(nq) for j in range(_last_key_block(i, tq, tk) + 1)]
    return (jnp.asarray(np.array([p[0] for p in pairs], np.int32)),
            jnp.asarray(np.array([p[1] for p in pairs], np.int32)))


def _online_softmax(u, m_old, v, rows_bias=None):
    m_blk = jnp.max(u, axis=-1, keepdims=True)
    if rows_bias is not None:
        m_blk = m_blk + rows_bias
    m_new = jnp.maximum(m_old, m_blk)
    shift = m_new if rows_bias is None else m_new - rows_bias
    p = jnp.exp2(u - shift)
    alpha = jnp.exp2(m_old - m_new)
    return m_new, alpha, jnp.sum(p, axis=-1, keepdims=True), jnp.dot(p.astype(BF16), v, preferred_element_type=F32)


def _fox_flash_kernel(qi_ref, ki_ref, q_ref, k_ref, v_ref, cq_ref, ck_ref, o_ref, m_sc, l_sc, acc_sc, *, tq, tk):
    t = pl.program_id(2)
    qi = qi_ref[t]
    ki = ki_ref[t]
    last = _last_key_block(qi, tq, tk)
    rows = FOX_GROUP * tq

    @pl.when(ki == 0)
    def _():
        m_sc[...] = jnp.full_like(m_sc, MASKED)
        l_sc[...] = jnp.zeros_like(l_sc)
        acc_sc[...] = jnp.zeros_like(acc_sc)

    def step(diagonal):
        if diagonal:
            qpos = qi * tq + lax.broadcasted_iota(I32, (tq, tk), 0)
            kpos = ki * tk + lax.broadcasted_iota(I32, (tq, tk), 1)
            visible = (kpos <= qpos)[None]
        for j in range(2):
            kj = k_ref[:, j * HEAD_DIM:(j + 1) * HEAD_DIM]
            vj = v_ref[:, j * HEAD_DIM:(j + 1) * HEAD_DIM]
            qs = jnp.concatenate([q_ref[:, (FOX_GROUP * j + g) * HEAD_DIM:(FOX_GROUP * j + g + 1) * HEAD_DIM]
                                  for g in range(FOX_GROUP)], axis=0)
            s = lax.dot_general(qs, kj, NT_DIMS, preferred_element_type=F32)
            u = s.reshape(FOX_GROUP, tq, tk) - ck_ref[FOX_GROUP * j:FOX_GROUP * (j + 1)]
            if diagonal:
                u = jnp.where(visible, u, MASKED)
            cq = cq_ref[FOX_GROUP * j:FOX_GROUP * (j + 1)].reshape(rows, 1)
            m_new, alpha, psum, pv = _online_softmax(u.reshape(rows, tk), m_sc[j], vj, cq)
            l_sc[j] = alpha * l_sc[j] + psum
            acc_sc[j] = alpha * acc_sc[j] + pv
            m_sc[j] = m_new

    @pl.when(ki < last)
    def _():
        step(False)

    @pl.when(ki == last)
    def _():
        step(True)
        for j in range(2):
            o = acc_sc[j] / l_sc[j]
            for g in range(FOX_GROUP):
                h = FOX_GROUP * j + g
                o_ref[:, h * HEAD_DIM:(h + 1) * HEAD_DIM] = o[g * tq:(g + 1) * tq, :]


def _fox_prompt_attention(q, k, v, cq, ck, batch, seq, tq, tk):
    assert tk % tq == 0
    nq, nk = seq // tq, seq // tk
    qi_tab, ki_tab = _triangle(nq, tq, tk)
    rows = FOX_GROUP * tq
    nhead = 2 * FOX_GROUP
    qmap = lambda b, hb, t, qi, ki: (b * nq + qi[t], hb)
    kmap = lambda b, hb, t, qi, ki: (b * nk + ki[t], hb)
    return pl.pallas_call(
        functools.partial(_fox_flash_kernel, tq=tq, tk=tk),
        grid_spec=pltpu.PrefetchScalarGridSpec(
            num_scalar_prefetch=2,
            grid=(batch, FOX_KV_HEADS // 2, qi_tab.shape[0]),
            in_specs=[
                pl.BlockSpec((tq, nhead * HEAD_DIM), qmap),
                pl.BlockSpec((tk, 2 * HEAD_DIM), kmap),
                pl.BlockSpec((tk, 2 * HEAD_DIM), kmap),
                pl.BlockSpec((None, nhead, tq, 1), lambda b, hb, t, qi, ki: (b, hb, qi[t], 0)),
                pl.BlockSpec((None, nhead, 1, tk), lambda b, hb, t, qi, ki: (b, hb, 0, ki[t])),
            ],
            out_specs=pl.BlockSpec((tq, nhead * HEAD_DIM), qmap),
            scratch_shapes=[pltpu.VMEM((2, rows, 1), F32), pltpu.VMEM((2, rows, 1), F32),
                            pltpu.VMEM((2, rows, HEAD_DIM), F32)],
        ),
        out_shape=jax.ShapeDtypeStruct((batch * seq, FOX_WIDTH), F32),
        compiler_params=_params("parallel", "parallel", "arbitrary"),
        name="fox_prompt_attn",
    )(qi_tab, ki_tab, q, k, v, cq, ck)


def _mla_flash_kernel(qi_ref, ki_ref, q_ref, k_ref, o_ref, m_sc, l_sc, acc_sc, *, tq, tk):
    t = pl.program_id(1)
    qi = qi_ref[t]
    ki = ki_ref[t]
    last = _last_key_block(qi, tq, tk)

    @pl.when(ki == 0)
    def _():
        m_sc[...] = jnp.full_like(m_sc, MASKED)
        l_sc[...] = jnp.zeros_like(l_sc)
        acc_sc[...] = jnp.zeros_like(acc_sc)

    def step(diagonal):
        kc = k_ref[...]
        vc = kc[:, 0:MLA_KV_RANK]
        if diagonal:
            qpos = qi * tq + lax.broadcasted_iota(I32, (tq, tk), 0)
            kpos = ki * tk + lax.broadcasted_iota(I32, (tq, tk), 1)
            visible = kpos <= qpos
        for h in range(MLA_HEADS):
            s = lax.dot_general(q_ref[h], kc, NT_DIMS, preferred_element_type=F32)
            if diagonal:
                s = jnp.where(visible, s, MASKED)
            m_new, alpha, psum, pv = _online_softmax(s, m_sc[h], vc)
            l_sc[h] = alpha * l_sc[h] + psum
            acc_sc[h] = alpha * acc_sc[h] + pv
            m_sc[h] = m_new

    @pl.when(ki < last)
    def _():
        step(False)

    @pl.when(ki == last)
    def _():
        step(True)
        for h in range(MLA_HEADS):
            o_ref[:, h * MLA_KV_RANK:(h + 1) * MLA_KV_RANK] = acc_sc[h] / l_sc[h]


def _mla_prompt_attention(q, kcat, batch, seq, tq, tk):
    assert tk % tq == 0
    nq, nk = seq // tq, seq // tk
    qi_tab, ki_tab = _triangle(nq, tq, tk)
    return pl.pallas_call(
        functools.partial(_mla_flash_kernel, tq=tq, tk=tk),
        grid_spec=pltpu.PrefetchScalarGridSpec(
            num_scalar_prefetch=2,
            grid=(batch, qi_tab.shape[0]),
            in_specs=[
                pl.BlockSpec((MLA_HEADS, tq, MLA_QK_DIM), lambda b, t, qi, ki: (0, b * nq + qi[t], 0)),
                pl.BlockSpec((tk, MLA_QK_DIM), lambda b, t, qi, ki: (b * nk + ki[t], 0)),
            ],
            out_specs=pl.BlockSpec((tq, MLA_HEADS * MLA_KV_RANK), lambda b, t, qi, ki: (b * nq + qi[t], 0)),
            scratch_shapes=[pltpu.VMEM((MLA_HEADS, tq, 1), F32), pltpu.VMEM((MLA_HEADS, tq, 1), F32),
                            pltpu.VMEM((MLA_HEADS, tq, MLA_KV_RANK), F32)],
        ),
        out_shape=jax.ShapeDtypeStruct((batch * seq, MLA_HEADS * MLA_KV_RANK), F32),
        compiler_params=_params("parallel", "arbitrary"),
        name="mla_prompt_attn",
    )(qi_tab, ki_tab, q, kcat)


PAGES_PER_STEP = 8


def _split3_bf16(x):
    hi = x.astype(BF16)
    r1 = x - hi.astype(F32)
    mid = r1.astype(BF16)
    lo = (r1 - mid.astype(F32)).astype(BF16)
    return hi, mid, lo


def _sample_attn_kernel(pt_ref, qf_ref, ql_ref, qr_ref, kn_ref, vn_ref, cn_ref, rn_ref, lfn_ref,
                        kt_hbm, vt_hbm, cc_hbm, rt_hbm, lt_hbm,
                        of_ref, om_ref,
                        kbuf, vbuf, cbuf, rbuf, lbuf, sems, m_sc, l_sc, accf_sc, accm_sc, run_sc, cnew_sc,
                        *, layer, n_pages, n_seq, page, dec_seq):
    b = pl.program_id(0)
    step = pl.program_id(1)
    pp = PAGES_PER_STEP
    nch = n_pages // pp
    nr = dec_seq * FOX_HEADS

    def chunk_copies(bb, cc, slot):
        out = []
        for j in range(pp):
            pg = pt_ref[bb * n_pages + n_pages - (cc + 1) * pp + j]
            for a, (hbm, buf) in enumerate(((kt_hbm, kbuf), (vt_hbm, vbuf), (cc_hbm, cbuf),
                                            (rt_hbm, rbuf), (lt_hbm, lbuf))):
                out.append(pltpu.make_async_copy(hbm.at[layer, pg], buf.at[slot, j], sems.at[a, slot]))
        return out

    @pl.when((b == 0) & (step == 0))
    def _():
        for cp in chunk_copies(0, 0, 0):
            cp.start()

    def attend_first(s, pv):
        m_new = jnp.max(s, axis=-1, keepdims=True)
        p = jnp.exp2(s - m_new)
        l_sc[...] = jnp.sum(p, axis=-1, keepdims=True)
        o_f, o_m = pv(p)
        accf_sc[...] = o_f
        accm_sc[...] = o_m
        m_sc[...] = m_new

    def attend_next(s, pv):
        m_old = m_sc[...]
        m_new = jnp.maximum(m_old, jnp.max(s, axis=-1, keepdims=True))
        p = jnp.exp2(s - m_new)
        alpha = jnp.exp2(m_old - m_new)
        l_sc[...] = alpha * l_sc[...] + jnp.sum(p, axis=-1, keepdims=True)
        o_f, o_m = pv(p)
        accf_sc[...] = alpha[0:nr] * accf_sc[...] + o_f
        accm_sc[...] = alpha[nr:2 * nr] * accm_sc[...] + o_m
        m_sc[...] = m_new

    @pl.when(step == 0)
    def _():
        lane8 = lax.broadcasted_iota(I32, (FOX_HEADS, page), 1)
        c = lfn_ref[...] * LOG2E
        shift = 1
        while shift < dec_seq:
            c = c + jnp.where(lane8 >= shift, pltpu.roll(c, shift, 1), 0.0)
            shift *= 2
        c_key = jnp.concatenate([c] * dec_seq, axis=0)
        c_new = jnp.concatenate([c[:, q:q + 1] for q in range(dec_seq)], axis=0)
        tok = lax.broadcasted_iota(I32, (nr, page), 0) // FOX_HEADS
        lane = lax.broadcasted_iota(I32, (nr, page), 1)
        visible = lane <= tok
        k = kn_ref[...].astype(BF16)
        v = vn_ref[...].astype(BF16)
        cm = cn_ref[...].astype(BF16)
        r = rn_ref[...].astype(BF16)
        s_f = lax.dot_general(qf_ref[...], k, NT_DIMS, preferred_element_type=F32) + (c_new - c_key)
        s_m = (lax.dot_general(ql_ref[...], cm, NT_DIMS, preferred_element_type=F32)
               + lax.dot_general(qr_ref[...], r, NT_DIMS, preferred_element_type=F32))
        s = jnp.concatenate([jnp.where(visible, s_f, MASKED), jnp.where(visible, s_m, MASKED)], axis=0)
        attend_first(s, lambda p: (jnp.dot(p[0:nr].astype(BF16), v, preferred_element_type=F32),
                                   jnp.dot(p[nr:2 * nr].astype(BF16), cm, preferred_element_type=F32)))
        run_sc[...] = jnp.zeros_like(run_sc)
        cnew_sc[...] = c_new

    @pl.when(step >= 1)
    def _():
        cc = step - 1
        n = b * nch + cc
        slot = lax.rem(n, 2)
        for cp in chunk_copies(b, cc, slot):
            cp.wait()

        @pl.when(n + 1 < n_seq * nch)
        def _():
            wrap = cc + 1 == nch
            nb = jnp.where(wrap, b + 1, b)
            nc = jnp.where(wrap, 0, cc + 1)
            for cp in chunk_copies(nb, nc, 1 - slot):
                cp.start()

        lf = lbuf[slot].reshape(pp * FOX_HEADS, page) * LOG2E
        strict = (lax.broadcasted_iota(I32, (page, page), 0)
                  > lax.broadcasted_iota(I32, (page, page), 1)).astype(BF16)
        within = sum(jnp.dot(part, strict, preferred_element_type=F32) for part in _split3_bf16(lf))
        tot = jnp.sum(lf, axis=-1, keepdims=True)
        later = run_sc[...]
        c_new = cnew_sc[...]
        qf = qf_ref[...]
        ql = ql_ref[...]
        qr = qr_ref[...]
        s_f, s_m, vts, cms = [None] * pp, [None] * pp, [], []
        for j in reversed(range(pp)):
            d = within[j * FOX_HEADS:(j + 1) * FOX_HEADS] + later
            later = later + tot[j * FOX_HEADS:(j + 1) * FOX_HEADS]
            kt = kbuf[slot, j].reshape(FOX_KV_WIDTH, page).astype(BF16)
            cm = cbuf[slot, j].astype(BF16)
            rt = rbuf[slot, j].astype(BF16)
            s_f[j] = (jnp.dot(qf, kt, preferred_element_type=F32)
                      + (jnp.concatenate([d] * dec_seq, axis=0) + c_new))
            s_m[j] = (lax.dot_general(ql, cm, NT_DIMS, preferred_element_type=F32)
                      + jnp.dot(qr, rt, preferred_element_type=F32))
        run_sc[...] = later
        s = jnp.concatenate([jnp.concatenate(s_f, axis=1), jnp.concatenate(s_m, axis=1)], axis=0)

        def pv(p):
            o_f = jnp.zeros((nr, FOX_KV_WIDTH), F32)
            o_m = jnp.zeros((nr, MLA_KV_RANK), F32)
            for j in range(pp):
                pj = p[:, j * page:(j + 1) * page].astype(BF16)
                vt = vbuf[slot, j].reshape(FOX_KV_WIDTH, page).astype(BF16)
                o_f = o_f + lax.dot_general(pj[0:nr], vt, NT_DIMS, preferred_element_type=F32)
                o_m = o_m + jnp.dot(pj[nr:2 * nr], cbuf[slot, j].astype(BF16), preferred_element_type=F32)
            return o_f, o_m

        attend_next(s, pv)

    @pl.when(step == nch)
    def _():
        l = l_sc[...]
        of_ref[...] = accf_sc[...] / l[0:nr]
        om_ref[...] = accm_sc[...] / l[nr:2 * nr]


def _sample_attention(layer, page_table, qf, ql, qr, kn, vn, cn, rn, lfn, kt, vt, cc, rt, lt):
    n_seq, n_pages = page_table.shape
    page = cc.shape[2]
    nr = qf.shape[1]
    dec_seq = nr // FOX_HEADS
    pp = PAGES_PER_STEP
    nch = n_pages // pp
    per_seq = lambda *tail: pl.BlockSpec((None,) + tail, lambda b, s, pt: (b,) + (0,) * len(tail))
    hbm = pl.BlockSpec(memory_space=pl.ANY)
    kern = functools.partial(_sample_attn_kernel, layer=layer, n_pages=n_pages, n_seq=n_seq, page=page,
                             dec_seq=dec_seq)
    return pl.pallas_call(
        kern,
        grid_spec=pltpu.PrefetchScalarGridSpec(
            num_scalar_prefetch=1,
            grid=(n_seq, nch + 1),
            in_specs=[per_seq(nr, FOX_KV_WIDTH), per_seq(nr, MLA_KV_RANK), per_seq(nr, MLA_ROPE_DIM),
                      per_seq(page, FOX_KV_WIDTH), per_seq(page, FOX_KV_WIDTH), per_seq(page, MLA_KV_RANK),
                      per_seq(page, MLA_ROPE_DIM), per_seq(FOX_HEADS, page),
                      hbm, hbm, hbm, hbm, hbm],
            out_specs=[per_seq(nr, FOX_KV_WIDTH), per_seq(nr, MLA_KV_RANK)],
            scratch_shapes=[
                pltpu.VMEM((2, pp, FOX_KV_HEADS, HEAD_DIM, page), F32),
                pltpu.VMEM((2, pp, FOX_KV_HEADS, HEAD_DIM, page), F32),
                pltpu.VMEM((2, pp, page, MLA_KV_RANK), F32), pltpu.VMEM((2, pp, MLA_ROPE_DIM, page), F32),
                pltpu.VMEM((2, pp, FOX_HEADS, page), F32), pltpu.SemaphoreType.DMA((5, 2)),
                pltpu.VMEM((2 * nr, 1), F32), pltpu.VMEM((2 * nr, 1), F32),
                pltpu.VMEM((nr, FOX_KV_WIDTH), F32), pltpu.VMEM((nr, MLA_KV_RANK), F32),
                pltpu.VMEM((FOX_HEADS, 1), F32), pltpu.VMEM((nr, 1), F32)],
        ),
        out_shape=[jax.ShapeDtypeStruct((n_seq, nr, FOX_KV_WIDTH), F32),
                   jax.ShapeDtypeStruct((n_seq, nr, MLA_KV_RANK), F32)],
        compiler_params=_params("arbitrary", "arbitrary"),
        name="sample_attn",
    )(page_table.reshape(-1), qf, ql, qr, kn, vn, cn, rn, lfn, kt, vt, cc, rt, lt)


def _merge_kernel(fo_ref, lat_ref, x_ref, gf_ref, gm_ref, wuv_ref, wo_ref, o_ref):
    fn = _rms(fo_ref[...], gf_ref[...])
    mo = jnp.dot(lat_ref[...].astype(BF16), wuv_ref[...], preferred_element_type=F32)
    mn = _rms(mo, gm_ref[...])
    y = (jnp.dot(fn.astype(BF16), wo_ref[0:FOX_WIDTH, :], preferred_element_type=F32)
         + jnp.dot(mn.astype(BF16), wo_ref[FOX_WIDTH:FOX_WIDTH + MLA_WIDTH, :], preferred_element_type=F32))
    o_ref[...] = x_ref[...] + y


def _merge(fo, lat, x, gf, gm, wuv, wo, tm):
    t = x.shape[0]
    row = lambda n: pl.BlockSpec((tm, n), lambda i: (i, 0))
    full = lambda a: pl.BlockSpec(a.shape, lambda i: (0,) * a.ndim)
    return pl.pallas_call(
        _merge_kernel,
        grid=(t // tm,),
        in_specs=[row(FOX_WIDTH), row(MLA_HEADS * MLA_KV_RANK), row(D_MODEL), full(gf), full(gm),
                  full(wuv), full(wo)],
        out_specs=row(D_MODEL),
        out_shape=jax.ShapeDtypeStruct((t, D_MODEL), F32),
        compiler_params=_params("parallel"),
        name="head_merge",
    )(fo, lat, x, gf, gm, wuv, wo)


def _swiglu_chunk(h, wg, wu, wd):
    gate = jnp.dot(h, wg, preferred_element_type=F32)
    up = jnp.dot(h, wu, preferred_element_type=F32)
    a = gate * (1.0 / (1.0 + jnp.exp(-gate))) * up
    return jnp.dot(a.astype(BF16), wd, preferred_element_type=F32)


def _ffn_kernel(x_ref, g_ref, wg_ref, wu_ref, wd_ref, o_ref, h_sc, acc_sc):
    f = pl.program_id(1)

    @pl.when(f == 0)
    def _():
        h_sc[...] = _rms(x_ref[...], g_ref[...]).astype(BF16)
        acc_sc[...] = jnp.zeros_like(acc_sc)

    acc_sc[...] += _swiglu_chunk(h_sc[...], wg_ref[...], wu_ref[...], wd_ref[...])

    @pl.when(f == pl.num_programs(1) - 1)
    def _():
        o_ref[...] = x_ref[...] + acc_sc[...]


def _ffn_dense(x, g, wg, wu, wd, tf, tm):
    t = x.shape[0]
    nf = wg.shape[1] // tf
    return pl.pallas_call(
        _ffn_kernel,
        grid=(t // tm, nf),
        in_specs=[pl.BlockSpec((tm, D_MODEL), lambda i, f: (i, 0)),
                  pl.BlockSpec((1, D_MODEL), lambda i, f: (0, 0)),
                  pl.BlockSpec((D_MODEL, tf), lambda i, f: (0, f)),
                  pl.BlockSpec((D_MODEL, tf), lambda i, f: (0, f)),
                  pl.BlockSpec((tf, D_MODEL), lambda i, f: (f, 0))],
        out_specs=pl.BlockSpec((tm, D_MODEL), lambda i, f: (i, 0)),
        out_shape=jax.ShapeDtypeStruct((t, D_MODEL), F32),
        scratch_shapes=[pltpu.VMEM((tm, D_MODEL), BF16), pltpu.VMEM((tm, D_MODEL), F32)],
        compiler_params=_params("parallel", "arbitrary"),
        name="ffn_dense",
    )(x, g, wg, wu, wd)


def _router_kernel(x_ref, g_ref, wr_ref, idx_ref, w_ref):
    h = _rms(x_ref[...], g_ref[...])
    logits = jnp.dot(h, wr_ref[...], precision=lax.Precision.HIGHEST, preferred_element_type=F32)
    lane = lax.broadcasted_iota(I32, logits.shape, 1)
    v1 = jnp.max(logits, axis=-1, keepdims=True)
    i1 = jnp.min(jnp.where(logits == v1, lane, N_EXPERTS), axis=-1, keepdims=True)
    rest = jnp.where(lane == i1, -jnp.inf, logits)
    v2 = jnp.max(rest, axis=-1, keepdims=True)
    i2 = jnp.min(jnp.where(rest == v2, lane, N_EXPERTS), axis=-1, keepdims=True)
    e2 = jnp.exp(v2 - v1)
    first = lax.broadcasted_iota(I32, idx_ref.shape, 1) == 0
    idx_ref[...] = jnp.where(first, i1, i2)
    w_ref[...] = jnp.where(first, 1.0 / (1.0 + e2), e2 / (1.0 + e2))


def _router(x, g, wr, tm):
    t = x.shape[0]
    return pl.pallas_call(
        _router_kernel,
        grid=(t // tm,),
        in_specs=[pl.BlockSpec((tm, D_MODEL), lambda i: (i, 0)),
                  pl.BlockSpec((1, D_MODEL), lambda i: (0, 0)),
                  pl.BlockSpec((D_MODEL, N_EXPERTS), lambda i: (0, 0))],
        out_specs=[pl.BlockSpec((tm, TOP_K), lambda i: (i, 0)), pl.BlockSpec((tm, TOP_K), lambda i: (i, 0))],
        out_shape=[jax.ShapeDtypeStruct((t, TOP_K), I32), jax.ShapeDtypeStruct((t, TOP_K), F32)],
        compiler_params=_params("parallel"),
        name="moe_router",
    )(x, g, wr)


MOE_ROW_TILE = 512
CHUNKS = D_MODEL // LANES


def _route_tables(idx, w, tmg):
    t = idx.shape[0]
    na = TOP_K * t
    a_exp = idx.reshape(na)
    onehot = (a_exp[:, None] == jnp.arange(N_EXPERTS, dtype=I32)[None, :]).astype(I32)
    csum = jnp.cumsum(onehot, axis=0)
    rank = jnp.take_along_axis(csum, a_exp[:, None], axis=1)[:, 0] - 1
    counts = csum[-1]
    tiles_per = (counts + tmg - 1) // tmg
    tile_end = jnp.cumsum(tiles_per)
    group_start = (tile_end - tiles_per) * tmg
    slot = group_start[a_exp] + rank
    nt = -(-na // tmg) + N_EXPERTS
    nslot = nt * tmg
    tok = jnp.arange(na, dtype=I32) // TOP_K
    k = jnp.arange(na, dtype=I32) % TOP_K
    slot_token = jnp.zeros((nslot,), I32).at[slot].set(tok)
    slot_w = jnp.zeros((nslot,), F32).at[slot].set(w.reshape(na))
    slot_dst = jnp.full((nslot,), -1, I32).at[slot].set(k * t + tok)
    n_used = tile_end[-1]
    tile_ids = jnp.arange(nt, dtype=I32)
    tile_expert = jnp.minimum(jnp.searchsorted(tile_end, jnp.minimum(tile_ids, n_used - 1), side='right'),
                              N_EXPERTS - 1).astype(I32)
    tile_valid = (tile_ids < n_used).astype(I32)
    return (tile_expert, tile_valid, slot_token.reshape(nt, 1, tmg), slot_dst.reshape(nt, 1, tmg),
            slot_w.reshape(nslot, 1))


def _moe_gmm_kernel(te_ref, tv_ref, tok_ref, ntok_ref, dst_ref, pdst_ref, x_hbm, g_ref, w_ref, wg_ref, wu_ref, wd_ref,
                    y_hbm, xbuf, ybuf, h_sc, acc_sc, gsem, ssem, *, tmg):
    i = pl.program_id(0)
    f = pl.program_id(1)
    nt = pl.num_programs(0)
    nf = pl.num_programs(1)
    slot = lax.rem(i, 2)

    def gather(tref, sl, start):
        def body(r, c):
            cp = pltpu.make_async_copy(x_hbm.at[tref[0, r]], xbuf.at[sl, pl.ds(r * CHUNKS, CHUNKS)], gsem.at[sl])
            cp.start() if start else cp.wait()
            return c
        lax.fori_loop(0, tmg, body, 0)

    def scatter(dref, start):
        def body(r, c):
            d = dref[0, r]

            @pl.when(d >= 0)
            def _():
                cp = pltpu.make_async_copy(ybuf.at[pl.ds(r * CHUNKS, CHUNKS)], y_hbm.at[d], ssem.at[0])
                cp.start() if start else cp.wait()
            return c
        lax.fori_loop(0, tmg, body, 0)

    @pl.when(f == 0)
    def _():
        @pl.when(i == 0)
        def _():
            gather(tok_ref, 0, True)

        gather(tok_ref, slot, False)

        @pl.when(i + 1 < nt)
        def _():
            gather(ntok_ref, 1 - slot, True)

    valid = tv_ref[i] == 1

    @pl.when(valid & (f == 0))
    def _():
        xs = [xbuf[slot, pl.ds(c, tmg, stride=CHUNKS), :] for c in range(CHUNKS)]
        ss = sum(jnp.sum(xc * xc, axis=-1, keepdims=True) for xc in xs)
        rs = lax.rsqrt(ss * (1.0 / D_MODEL) + NORM_EPS)
        for c in range(CHUNKS):
            h_sc[:, c * LANES:(c + 1) * LANES] = (xs[c] * rs * g_ref[:, c * LANES:(c + 1) * LANES]).astype(BF16)
        acc_sc[...] = jnp.zeros_like(acc_sc)

    @pl.when(valid)
    def _():
        acc_sc[...] += _swiglu_chunk(h_sc[...], wg_ref[...], wu_ref[...], wd_ref[...])

    @pl.when(valid & (f == nf - 1))
    def _():
        @pl.when(i > 0)
        def _():
            scatter(pdst_ref, False)

        y = acc_sc[...] * w_ref[...]
        for c in range(CHUNKS):
            ybuf[pl.ds(c, tmg, stride=CHUNKS), :] = y[:, c * LANES:(c + 1) * LANES]
        scatter(dst_ref, True)

        last_valid = jnp.where(i + 1 < nt, tv_ref[jnp.minimum(i + 1, nt - 1)], 0) == 0

        @pl.when(last_valid)
        def _():
            scatter(dst_ref, False)


def _moe_gmm(x3, g, tables, wg, wu, wd, tf, tmg):
    tile_expert, tile_valid, slot_token, slot_dst, slot_w = tables
    t = x3.shape[0]
    nt = slot_token.shape[0]
    ne, _, ff = wg.shape
    nf = ff // tf
    smem_tile = lambda fn: pl.BlockSpec((None, 1, tmg), fn, memory_space=pltpu.SMEM)
    return pl.pallas_call(
        functools.partial(_moe_gmm_kernel, tmg=tmg),
        grid_spec=pltpu.PrefetchScalarGridSpec(
            num_scalar_prefetch=2,
            grid=(nt, nf),
            in_specs=[smem_tile(lambda i, f, te, tv: (i, 0, 0)),
                      smem_tile(lambda i, f, te, tv: (jnp.minimum(i + 1, nt - 1), 0, 0)),
                      smem_tile(lambda i, f, te, tv: (i, 0, 0)),
                      smem_tile(lambda i, f, te, tv: (jnp.maximum(i - 1, 0), 0, 0)),
                      pl.BlockSpec(memory_space=pl.ANY),
                      pl.BlockSpec((1, D_MODEL), lambda i, f, te, tv: (0, 0)),
                      pl.BlockSpec((tmg, 1), lambda i, f, te, tv: (i, 0)),
                      pl.BlockSpec((None, D_MODEL, tf), lambda i, f, te, tv: (te[i], 0, f)),
                      pl.BlockSpec((None, D_MODEL, tf), lambda i, f, te, tv: (te[i], 0, f)),
                      pl.BlockSpec((None, tf, D_MODEL), lambda i, f, te, tv: (te[i], f, 0))],
            out_specs=pl.BlockSpec(memory_space=pl.ANY),
            scratch_shapes=[pltpu.VMEM((2, tmg * CHUNKS, LANES), F32), pltpu.VMEM((tmg * CHUNKS, LANES), F32),
                            pltpu.VMEM((tmg, D_MODEL), BF16), pltpu.VMEM((tmg, D_MODEL), F32),
                            pltpu.SemaphoreType.DMA((2,)), pltpu.SemaphoreType.DMA((1,))],
        ),
        out_shape=jax.ShapeDtypeStruct((TOP_K * t, CHUNKS, LANES), F32),
        compiler_params=_params("arbitrary", "arbitrary"),
        name="moe_experts",
    )(tile_expert, tile_valid, slot_token, slot_token, slot_dst, slot_dst, x3, g, slot_w, wg, wu, wd)


def _moe_combine_kernel(x_ref, y0_ref, y1_ref, o_ref):
    o_ref[...] = (x_ref[...] + y0_ref[...]) + y1_ref[...]


def _moe_combine(x3, y3, tm):
    t = x3.shape[0]
    nb = t // tm
    blk = lambda fn: pl.BlockSpec((tm, CHUNKS, LANES), fn)
    return pl.pallas_call(
        _moe_combine_kernel,
        grid=(nb,),
        in_specs=[blk(lambda i: (i, 0, 0)), blk(lambda i: (i, 0, 0)), blk(lambda i: (i + nb, 0, 0))],
        out_specs=blk(lambda i: (i, 0, 0)),
        out_shape=jax.ShapeDtypeStruct(x3.shape, F32),
        compiler_params=_params("parallel"),
        name="moe_combine",
    )(x3, y3, y3)


def _final_norm_kernel(x_ref, g_ref, o_ref):
    o_ref[...] = _rms(x_ref[...], g_ref[...])


def _final_norm(x, g, tm):
    t = x.shape[0]
    return pl.pallas_call(
        _final_norm_kernel,
        grid=(t // tm,),
        in_specs=[pl.BlockSpec((tm, D_MODEL), lambda i: (i, 0)), pl.BlockSpec((1, D_MODEL), lambda i: (0, 0))],
        out_specs=pl.BlockSpec((tm, D_MODEL), lambda i: (i, 0)),
        out_shape=jax.ShapeDtypeStruct((t, D_MODEL), F32),
        compiler_params=_params("parallel"),
        name="final_norm",
    )(x, g)


def _rotate_half_cols(w):
    half = w.shape[-1] // 2
    return jnp.concatenate([-w[..., half:], w[..., :half]], axis=-1)


def _prep_w_in(w):
    ends = [FOX_WIDTH, FOX_WIDTH + FOX_KV_WIDTH, FOX_WIDTH + 2 * FOX_KV_WIDTH]
    ends.append(ends[-1] + FOX_HEADS)
    ends.append(ends[-1] + MLA_Q_RANK)
    ends.append(ends[-1] + MLA_KV_RANK)
    fq, fk, fv, fz, cq, ckv, kr = jnp.split(w, ends, axis=-1)
    pad = jnp.zeros((w.shape[0], LANES - 2 * MLA_ROPE_DIM - FOX_HEADS), w.dtype)
    return jnp.concatenate([fq * (FOX_SCALE * LOG2E), fk, fv, cq, ckv, kr, _rotate_half_cols(kr), fz, pad],
                           axis=-1).astype(BF16)


def _prep_w_uq(w_uq):
    nope = w_uq[:, :, :MLA_NOPE_DIM]
    rope = w_uq[:, :, MLA_NOPE_DIM:]
    padto = lambda a: jnp.pad(a, ((0, 0), (0, 0), (0, LANES - a.shape[-1])))
    w = jnp.concatenate([padto(nope), padto(rope), padto(_rotate_half_cols(rope))], axis=-1)
    return jnp.transpose(w, (1, 0, 2)).astype(BF16)


def _prep_w_uk(w_uk):
    w = jnp.transpose(w_uk, (1, 2, 0))
    return jnp.pad(w, ((0, 0), (0, LANES - w.shape[1]), (0, 0))).astype(BF16)


def _prep_w_uv(w_uv):
    c, h, v = w_uv.shape
    bd = jnp.einsum('chv,hg->hcgv', w_uv, jnp.eye(h, dtype=w_uv.dtype))
    return bd.reshape(h * c, h * v).astype(BF16)


def _rope_tables(pos):
    half = MLA_ROPE_DIM // 2
    inv_freq = jnp.power(ROPE_THETA, -jnp.arange(half, dtype=F32) / half)
    ang = pos.astype(F32)[:, None] * inv_freq[None, :]
    pad = lambda a: jnp.pad(jnp.concatenate([a, a], axis=-1), ((0, 0), (0, LANES - MLA_ROPE_DIM)))
    return pad(jnp.cos(ang)), pad(jnp.sin(ang))


def kernel(x_prompt, x_sample, cache_fox_k, cache_fox_v, cache_fox_logf, cache_mla_ckv, cache_mla_krope, page_table, w_in, b_f, g_attn, g_cq, g_ckv, w_uq, w_uk, w_uv, g_fox_out, g_mla_out, w_o, g_ffn, w_gate_dense, w_up_dense, w_down_dense, w_router, w_gate_exp, w_up_exp, w_down_exp, g_final):
    batch, seq, _ = x_prompt.shape
    n_seq, dec_seq, _ = x_sample.shape
    depth, n_pool, page = cache_fox_k.shape[:3]
    n_pages = page_table.shape[1]
    past_len = n_pages * page
    tp = batch * seq
    ts = n_seq * dec_seq
    t_all = tp + ts
    tm = _largest_tile(TOKEN_TILE, tp, ts)
    tk = _largest_tile(512, seq)
    tmg = _largest_tile(MOE_ROW_TILE, t_all)
    nr = dec_seq * FOX_HEADS
    assert n_pages % PAGES_PER_STEP == 0 and dec_seq <= page

    x = jnp.concatenate([x_prompt.reshape(tp, D_MODEL), x_sample.reshape(ts, D_MODEL)], axis=0)
    pos = jnp.concatenate([jnp.tile(jnp.arange(seq, dtype=I32), batch),
                           past_len + jnp.tile(jnp.arange(dec_seq, dtype=I32), n_seq)])
    cos, sin = _rope_tables(pos)

    kt = jnp.transpose(cache_fox_k, (0, 1, 3, 4, 2))
    vt = jnp.transpose(cache_fox_v, (0, 1, 3, 4, 2))
    rt = jnp.transpose(cache_mla_krope, (0, 1, 3, 2))
    lt = jnp.transpose(cache_fox_logf, (0, 1, 3, 2))

    row = lambda a: a.reshape(1, -1)
    outs = [[] for _ in range(10)]
    eye_kv = jnp.eye(FOX_KV_HEADS, dtype=BF16)
    pad_rows = lambda a: jnp.pad(a.reshape(n_seq, dec_seq, -1), ((0, 0), (0, page - dec_seq), (0, 0)))

    for l in range(depth):
        fq, fk, fv, fkb, fvb, cq, ckv, kr, kcat, lf = _proj(
            x, row(g_attn[l]), _prep_w_in(w_in[l]), row(g_ckv[l]), row(b_f[l]), cos, sin, tm)
        qcat = _mla_q(cq, row(g_cq[l]), _prep_w_uq(w_uq[l]), _prep_w_uk(w_uk[l]), cos, sin, tm)

        c = _cumsum(lf[:tp].reshape(batch, seq, FOX_HEADS), tk)
        c_t = jnp.transpose(c, (0, 2, 1))
        fo_p = _fox_prompt_attention(fq, fkb, fvb, c_t[:, :, :, None], c_t[:, :, None, :], batch, seq,
                                     _largest_tile(256, seq), tk)
        lat_p = _mla_prompt_attention(qcat, kcat, batch, seq, _largest_tile(128, seq), tk)

        q5 = fq[tp:].reshape(n_seq, dec_seq, FOX_KV_HEADS, FOX_GROUP, HEAD_DIM)
        qf = jnp.einsum('bqkgd,kj->bqkgjd', q5, eye_kv).reshape(n_seq, nr, FOX_KV_WIDTH)
        qm = jnp.transpose(qcat[:, tp:].reshape(MLA_HEADS, n_seq, dec_seq, MLA_QK_DIM),
                           (1, 2, 0, 3)).reshape(n_seq, nr, MLA_QK_DIM)
        lfn = jnp.transpose(pad_rows(lf[tp:]), (0, 2, 1))
        of, om = _sample_attention(l, page_table, qf, qm[:, :, :MLA_KV_RANK], qm[:, :, MLA_KV_RANK:],
                                   pad_rows(fk[tp:]), pad_rows(fv[tp:]), pad_rows(ckv[tp:]), pad_rows(kr[tp:]),
                                   lfn, kt, vt, cache_mla_ckv, rt, lt)
        of6 = of.reshape(n_seq, dec_seq, FOX_KV_HEADS, FOX_GROUP, FOX_KV_HEADS, HEAD_DIM)
        fo_s = jnp.einsum('bqkgjd,kj->bqkgd', of6, jnp.eye(FOX_KV_HEADS, dtype=F32)).reshape(ts, FOX_WIDTH)
        lat_s = om.reshape(ts, MLA_HEADS * MLA_KV_RANK)

        x = _merge(jnp.concatenate([fo_p, fo_s], axis=0), jnp.concatenate([lat_p, lat_s], axis=0), x,
                   row(g_fox_out[l]), row(g_mla_out[l]), _prep_w_uv(w_uv[l]), w_o[l].astype(BF16), tm)

        i = l // 2
        if l % 2 == 0:
            x = _ffn_dense(x, row(g_ffn[l]), w_gate_dense[i].astype(BF16), w_up_dense[i].astype(BF16),
                           w_down_dense[i].astype(BF16), _ff_tile(w_gate_dense.shape[-1]), tm)
        else:
            idx, wts = _router(x, row(g_ffn[l]), w_router[i], tm)
            x3 = x.reshape(t_all, CHUNKS, LANES)
            y3 = _moe_gmm(x3, row(g_ffn[l]), _route_tables(idx, wts, tmg), w_gate_exp[i].astype(BF16),
                          w_up_exp[i].astype(BF16), w_down_exp[i].astype(BF16),
                          _ff_tile(w_gate_exp.shape[-1]), tmg)
            x = _moe_combine(x3, y3, tmg).reshape(t_all, D_MODEL)

        for dst, a in zip(outs, (fk[:tp], fv[:tp], lf[:tp], ckv[:tp], kr[:tp],
                                 fk[tp:], fv[tp:], lf[tp:], ckv[tp:], kr[tp:])):
            dst.append(a)

    y = _final_norm(x, row(g_final), tm)
    shapes = [(batch, seq, FOX_KV_HEADS, HEAD_DIM), (batch, seq, FOX_KV_HEADS, HEAD_DIM), (batch, seq, FOX_HEADS),
              (batch, seq, MLA_KV_RANK), (batch, seq, MLA_ROPE_DIM),
              (n_seq, dec_seq, FOX_KV_HEADS, HEAD_DIM), (n_seq, dec_seq, FOX_KV_HEADS, HEAD_DIM),
              (n_seq, dec_seq, FOX_HEADS), (n_seq, dec_seq, MLA_KV_RANK), (n_seq, dec_seq, MLA_ROPE_DIM)]
    caches = tuple(jnp.stack(o).reshape((depth,) + s) for o, s in zip(outs, shapes))
    return (y[:tp].reshape(batch, seq, D_MODEL), y[tp:].reshape(n_seq, dec_seq, D_MODEL)) + caches
```

```python
import functools
import math

import jax
import jax.numpy as jnp
import numpy as np
from jax import lax
from jax.experimental import pallas as pl
from jax.experimental.pallas import tpu as pltpu

F32 = jnp.float32
BF16 = jnp.bfloat16
I32 = jnp.int32

D_MODEL = 1024
HEAD_DIM = 64
FOX_HEADS = 8
FOX_KV_HEADS = 4
FOX_GROUP = FOX_HEADS // FOX_KV_HEADS
FOX_WIDTH = FOX_HEADS * HEAD_DIM
FOX_KV_WIDTH = FOX_KV_HEADS * HEAD_DIM
MLA_HEADS = 8
MLA_NOPE_DIM = 64
MLA_ROPE_DIM = 32
MLA_V_DIM = 64
MLA_Q_RANK = 256
MLA_KV_RANK = 128
MLA_WIDTH = MLA_HEADS * MLA_V_DIM
MLA_QK_DIM = MLA_KV_RANK + MLA_ROPE_DIM
N_EXPERTS = 8
TOP_K = 2
ROPE_THETA = 10000.0
NORM_EPS = 1e-6
LOG2E = math.log2(math.e)
FOX_SCALE = HEAD_DIM ** -0.5
MLA_SCALE = (MLA_NOPE_DIM + MLA_ROPE_DIM) ** -0.5
MASKED = -1e30

LANES = 128
SUBLANES = 8
VMEM_LIMIT_BYTES = 48 * 1024 * 1024

C_FQ = 0
C_FK = C_FQ + FOX_WIDTH
C_FV = C_FK + FOX_KV_WIDTH
C_CQ = C_FV + FOX_KV_WIDTH
C_CKV = C_CQ + MLA_Q_RANK
C_MISC = C_CKV + MLA_KV_RANK
PROJ_COLS = C_MISC + LANES
M_KRR = MLA_ROPE_DIM
M_FZ = 2 * MLA_ROPE_DIM

TOKEN_TILE = 512
NT_DIMS = (((1,), (1,)), ((), ()))


def _largest_tile(cap, *sizes):
    t = cap
    while any(s % t for s in sizes):
        t //= 2
    return t


FF_TILE_CAP = 1536


def _ff_tile(ff):
    return max(c for c in range(LANES, FF_TILE_CAP + 1, LANES) if ff % c == 0)


def _rms(x, g):
    return x * lax.rsqrt(jnp.mean(x * x, axis=-1, keepdims=True) + NORM_EPS) * g


def _params(*sem):
    return pltpu.CompilerParams(dimension_semantics=sem, vmem_limit_bytes=VMEM_LIMIT_BYTES)


def _proj_kernel(x_ref, g_ref, w_ref, gckv_ref, bf_ref, cos_ref, sin_ref,
                 fq_ref, fk_ref, fv_ref, fkb_ref, fvb_ref, cq_ref, ckv_ref, kr_ref, kcat_ref, lf_ref):
    xn = _rms(x_ref[...], g_ref[...])
    p = jnp.dot(xn.astype(BF16), w_ref[...], preferred_element_type=F32)
    fq_ref[...] = p[:, C_FQ:C_FK].astype(BF16)
    fk = p[:, C_FK:C_FV]
    fv = p[:, C_FV:C_CQ]
    fk_ref[...] = fk
    fv_ref[...] = fv
    fkb_ref[...] = fk.astype(BF16)
    fvb_ref[...] = fv.astype(BF16)
    cq_ref[...] = p[:, C_CQ:C_CKV]
    ckv = _rms(p[:, C_CKV:C_MISC], gckv_ref[...])
    ckv_ref[...] = ckv
    misc = p[:, C_MISC:PROJ_COLS]
    rot = misc * cos_ref[...] + pltpu.roll(misc, LANES - M_KRR, 1) * sin_ref[...]
    kr = rot[:, 0:MLA_ROPE_DIM]
    kr_ref[...] = kr
    kcat_ref[:, 0:MLA_KV_RANK] = ckv.astype(BF16)
    kcat_ref[:, MLA_KV_RANK:MLA_QK_DIM] = kr.astype(BF16)
    z = pltpu.roll(misc, LANES - M_FZ, 1)[:, 0:FOX_HEADS] + bf_ref[...]
    lf_ref[...] = jnp.minimum(z, 0.0) - jnp.log1p(jnp.exp(-jnp.abs(z)))


def _proj(x, g, w, gckv, bf, cos, sin, tm):
    t = x.shape[0]
    row = lambda n: pl.BlockSpec((tm, n), lambda i: (i, 0))
    full = lambda a: pl.BlockSpec(a.shape, lambda i: (0,) * a.ndim)
    outs = [(FOX_WIDTH, BF16), (FOX_KV_WIDTH, F32), (FOX_KV_WIDTH, F32), (FOX_KV_WIDTH, BF16),
            (FOX_KV_WIDTH, BF16), (MLA_Q_RANK, F32), (MLA_KV_RANK, F32), (MLA_ROPE_DIM, F32),
            (MLA_QK_DIM, BF16), (FOX_HEADS, F32)]
    return pl.pallas_call(
        _proj_kernel,
        grid=(t // tm,),
        in_specs=[row(D_MODEL), full(g), full(w), full(gckv), full(bf), row(LANES), row(LANES)],
        out_specs=[row(n) for n, _ in outs],
        out_shape=[jax.ShapeDtypeStruct((t, n), d) for n, d in outs],
        compiler_params=_params("parallel"),
        name="in_proj",
    )(x, g, w, gckv, bf, cos, sin)


def _mlaq_kernel(cq_ref, g_ref, wq_ref, wuk_ref, cos_ref, sin_ref, o_ref):
    cqn = _rms(cq_ref[...], g_ref[...]).astype(BF16)
    scale = MLA_SCALE * LOG2E
    for h in range(MLA_HEADS):
        qa = jnp.dot(cqn, wq_ref[h], preferred_element_type=F32)
        lat = jnp.dot(qa[:, 0:LANES].astype(BF16), wuk_ref[h], preferred_element_type=F32)
        rot = qa[:, LANES:2 * LANES] * cos_ref[...] + qa[:, 2 * LANES:3 * LANES] * sin_ref[...]
        o_ref[h, :, 0:MLA_KV_RANK] = (lat * scale).astype(BF16)
        o_ref[h, :, MLA_KV_RANK:MLA_QK_DIM] = (rot[:, 0:MLA_ROPE_DIM] * scale).astype(BF16)


def _mla_q(cq, g, wq, wuk, cos, sin, tm):
    t = cq.shape[0]
    row = lambda n: pl.BlockSpec((tm, n), lambda i: (i, 0))
    full = lambda a: pl.BlockSpec(a.shape, lambda i: (0,) * a.ndim)
    return pl.pallas_call(
        _mlaq_kernel,
        grid=(t // tm,),
        in_specs=[row(MLA_Q_RANK), full(g), full(wq), full(wuk), row(LANES), row(LANES)],
        out_specs=pl.BlockSpec((MLA_HEADS, tm, MLA_QK_DIM), lambda i: (0, i, 0)),
        out_shape=jax.ShapeDtypeStruct((MLA_HEADS, t, MLA_QK_DIM), BF16),
        compiler_params=_params("parallel"),
        name="mla_q",
    )(cq, g, wq, wuk, cos, sin)


def _cumsum_kernel(lf_ref, o_ref, carry_sc):
    @pl.when(pl.program_id(1) == 0)
    def _():
        carry_sc[...] = jnp.zeros_like(carry_sc)

    n = lf_ref.shape[0]
    tri = (lax.broadcasted_iota(I32, (n, n), 1) <= lax.broadcasted_iota(I32, (n, n), 0)).astype(F32)
    c = jnp.dot(tri, lf_ref[...], precision=lax.Precision.HIGHEST, preferred_element_type=F32) + carry_sc[...]
    o_ref[...] = c * LOG2E
    carry_sc[...] = c[n - 1:n, :]


def _cumsum(lf, tile):
    b, s, h = lf.shape
    spec = pl.BlockSpec((None, tile, h), lambda i, j: (i, j, 0))
    return pl.pallas_call(
        _cumsum_kernel,
        grid=(b, s // tile),
        in_specs=[spec],
        out_specs=spec,
        out_shape=jax.ShapeDtypeStruct(lf.shape, F32),
        scratch_shapes=[pltpu.VMEM((1, h), F32)],
        compiler_params=_params("parallel", "arbitrary"),
        name="logf_cumsum",
    )(lf)


def _last_key_block(qi, tq, tk):
    return ((qi + 1) * tq - 1) // tk


def _triangle(nq, tq, tk):
    pairs = [(i, j) for i in range(nq) for j in range(_last_key_block(i, tq, tk) + 1)]
    return (jnp.asarray(np.array([p[0] for p in pairs], np.int32)),
            jnp.asarray(np.array([p[1] for p in pairs], np.int32)))


def _online_softmax(u, m_old, v, rows_bias=None):
    m_blk = jnp.max(u, axis=-1, keepdims=True)
    if rows_bias is not None:
        m_blk = m_blk + rows_bias
    m_new = jnp.maximum(m_old, m_blk)
    shift = m_new if rows_bias is None else m_new - rows_bias
    p = jnp.exp2(u - shift)
    alpha = jnp.exp2(m_old - m_new)
    return m_new, alpha, jnp.sum(p, axis=-1, keepdims=True), jnp.dot(p.astype(BF16), v, preferred_element_type=F32)


FOX_ROW_SPLIT = 1
MLA_LOOKAHEAD = 1
FOX_Q_TILE = 512
MLA_Q_TILE = 256


def _fox_flash_kernel(qi_ref, ki_ref, q_ref, k_ref, v_ref, cq_ref, ck_ref, o_ref, m_sc, l_sc, acc_sc, *, tq, tk):
    t = pl.program_id(2)
    qi = qi_ref[t]
    ki = ki_ref[t]
    last = _last_key_block(qi, tq, tk)

    @pl.when(ki == 0)
    def _():
        m_sc[...] = jnp.full_like(m_sc, MASKED)
        l_sc[...] = jnp.zeros_like(l_sc)
        acc_sc[...] = jnp.zeros_like(acc_sc)

    def step(diagonal):
        tr = tq // FOX_ROW_SPLIT
        if diagonal:
            qpos = qi * tq + lax.broadcasted_iota(I32, (tq, tk), 0)
            kpos = ki * tk + lax.broadcasted_iota(I32, (tq, tk), 1)
            visible = kpos <= qpos
        m_all = m_sc[...]
        l_all = l_sc[...]
        acc_all = acc_sc[...]
        res = []
        nchain = 2 * FOX_GROUP * FOX_ROW_SPLIT
        kv = lambda ref, h: ref[:, (h // FOX_GROUP) * HEAD_DIM:(h // FOX_GROUP + 1) * HEAD_DIM]

        def scores(c):
            h, part = divmod(c, FOX_ROW_SPLIT)
            return lax.dot_general(q_ref[part * tr:(part + 1) * tr, h * HEAD_DIM:(h + 1) * HEAD_DIM],
                                   kv(k_ref, h), NT_DIMS, preferred_element_type=F32)

        s_next = scores(0)
        for c in range(nchain):
            h, part = divmod(c, FOX_ROW_SPLIT)
            s = s_next
            if c + 1 < nchain:
                s_next = scores(c + 1)
            u = s - ck_ref[h]
            if diagonal:
                u = jnp.where(visible[part * tr:(part + 1) * tr], u, MASKED)
            m_new, alpha, psum, pv = _online_softmax(u, m_all[c], kv(v_ref, h),
                                                     cq_ref[h, part * tr:(part + 1) * tr])
            res.append((m_new, alpha * l_all[c] + psum, alpha * acc_all[c] + pv))
        m_sc[...] = jnp.stack([r[0] for r in res])
        l_sc[...] = jnp.stack([r[1] for r in res])
        acc_sc[...] = jnp.stack([r[2] for r in res])

    @pl.when(ki < last)
    def _():
        step(False)

    @pl.when(ki == last)
    def _():
        step(True)
        tr = tq // FOX_ROW_SPLIT
        for c in range(2 * FOX_GROUP * FOX_ROW_SPLIT):
            h, part = divmod(c, FOX_ROW_SPLIT)
            o_ref[part * tr:(part + 1) * tr, h * HEAD_DIM:(h + 1) * HEAD_DIM] = acc_sc[c] / l_sc[c]


def _fox_prompt_attention(q, k, v, cq, ck, batch, seq, tq, tk):
    assert tk % tq == 0
    nq, nk = seq // tq, seq // tk
    qi_tab, ki_tab = _triangle(nq, tq, tk)
    nhead = 2 * FOX_GROUP
    nchain, tr = nhead * FOX_ROW_SPLIT, tq // FOX_ROW_SPLIT
    qmap = lambda b, hb, t, qi, ki: (b * nq + qi[t], hb)
    kmap = lambda b, hb, t, qi, ki: (b * nk + ki[t], hb)
    return pl.pallas_call(
        functools.partial(_fox_flash_kernel, tq=tq, tk=tk),
        grid_spec=pltpu.PrefetchScalarGridSpec(
            num_scalar_prefetch=2,
            grid=(batch, FOX_KV_HEADS // 2, qi_tab.shape[0]),
            in_specs=[
                pl.BlockSpec((tq, nhead * HEAD_DIM), qmap),
                pl.BlockSpec((tk, 2 * HEAD_DIM), kmap),
                pl.BlockSpec((tk, 2 * HEAD_DIM), kmap),
                pl.BlockSpec((None, nhead, tq, 1), lambda b, hb, t, qi, ki: (b, hb, qi[t], 0)),
                pl.BlockSpec((None, nhead, 1, tk), lambda b, hb, t, qi, ki: (b, hb, 0, ki[t])),
            ],
            out_specs=pl.BlockSpec((tq, nhead * HEAD_DIM), qmap),
            scratch_shapes=[pltpu.VMEM((nchain, tr, 1), F32), pltpu.VMEM((nchain, tr, 1), F32),
                            pltpu.VMEM((nchain, tr, HEAD_DIM), F32)],
        ),
        out_shape=jax.ShapeDtypeStruct((batch * seq, FOX_WIDTH), F32),
        compiler_params=_params("parallel", "parallel", "arbitrary"),
        name="fox_prompt_attn",
    )(qi_tab, ki_tab, q, k, v, cq, ck)


def _mla_flash_kernel(qi_ref, ki_ref, q_ref, k_ref, o_ref, m_sc, l_sc, acc_sc, *, tq, tk):
    t = pl.program_id(1)
    qi = qi_ref[t]
    ki = ki_ref[t]
    last = _last_key_block(qi, tq, tk)

    @pl.when(ki == 0)
    def _():
        m_sc[...] = jnp.full_like(m_sc, MASKED)
        l_sc[...] = jnp.zeros_like(l_sc)
        acc_sc[...] = jnp.zeros_like(acc_sc)

    def step(diagonal):
        kc = k_ref[...]
        vc = kc[:, 0:MLA_KV_RANK]
        if diagonal:
            qpos = qi * tq + lax.broadcasted_iota(I32, (tq, tk), 0)
            kpos = ki * tk + lax.broadcasted_iota(I32, (tq, tk), 1)
            visible = kpos <= qpos
        m_all = m_sc[...]
        l_all = l_sc[...]
        acc_all = acc_sc[...]
        res = []
        scores = lambda h: lax.dot_general(q_ref[h], kc, NT_DIMS, preferred_element_type=F32)
        ahead = [scores(h) for h in range(MLA_LOOKAHEAD)]
        for h in range(MLA_HEADS):
            s = ahead.pop(0)
            if h + MLA_LOOKAHEAD < MLA_HEADS:
                ahead.append(scores(h + MLA_LOOKAHEAD))
            if diagonal:
                s = jnp.where(visible, s, MASKED)
            m_new, alpha, psum, pv = _online_softmax(s, m_all[h], vc)
            res.append((m_new, alpha * l_all[h] + psum, alpha * acc_all[h] + pv))
        m_sc[...] = jnp.stack([r[0] for r in res])
        l_sc[...] = jnp.stack([r[1] for r in res])
        acc_sc[...] = jnp.stack([r[2] for r in res])

    @pl.when(ki < last)
    def _():
        step(False)

    @pl.when(ki == last)
    def _():
        step(True)
        for h in range(MLA_HEADS):
            o_ref[:, h * MLA_KV_RANK:(h + 1) * MLA_KV_RANK] = acc_sc[h] / l_sc[h]


def _mla_prompt_attention(q, kcat, batch, seq, tq, tk):
    assert tk % tq == 0
    nq, nk = seq // tq, seq // tk
    qi_tab, ki_tab = _triangle(nq, tq, tk)
    return pl.pallas_call(
        functools.partial(_mla_flash_kernel, tq=tq, tk=tk),
        grid_spec=pltpu.PrefetchScalarGridSpec(
            num_scalar_prefetch=2,
            grid=(batch, qi_tab.shape[0]),
            in_specs=[
                pl.BlockSpec((MLA_HEADS, tq, MLA_QK_DIM), lambda b, t, qi, ki: (0, b * nq + qi[t], 0)),
                pl.BlockSpec((tk, MLA_QK_DIM), lambda b, t, qi, ki: (b * nk + ki[t], 0)),
            ],
            out_specs=pl.BlockSpec((tq, MLA_HEADS * MLA_KV_RANK), lambda b, t, qi, ki: (b * nq + qi[t], 0)),
            scratch_shapes=[pltpu.VMEM((MLA_HEADS, tq, 1), F32), pltpu.VMEM((MLA_HEADS, tq, 1), F32),
                            pltpu.VMEM((MLA_HEADS, tq, MLA_KV_RANK), F32)],
        ),
        out_shape=jax.ShapeDtypeStruct((batch * seq, MLA_HEADS * MLA_KV_RANK), F32),
        compiler_params=_params("parallel", "arbitrary"),
        name="mla_prompt_attn",
    )(qi_tab, ki_tab, q, kcat)


PAGES_PER_STEP = 16


def _split3_bf16(x):
    hi = x.astype(BF16)
    r1 = x - hi.astype(F32)
    mid = r1.astype(BF16)
    lo = (r1 - mid.astype(F32)).astype(BF16)
    return hi, mid, lo


def _sample_attn_kernel(pt_ref, qf_ref, ql_ref, qr_ref, kn_ref, vn_ref, cn_ref, rn_ref, lfn_ref,
                        kt_hbm, vt_hbm, cc_hbm, rt_hbm, lt_hbm,
                        of_ref, om_ref,
                        kbuf, vbuf, cbuf, rbuf, lbuf, sems, m_sc, l_sc, accf_sc, accm_sc, run_sc, cnew_sc,
                        *, layer, n_pages, n_seq, page, dec_seq):
    b = pl.program_id(0)
    step = pl.program_id(1)
    pp = PAGES_PER_STEP
    nch = n_pages // pp
    nr = dec_seq * FOX_HEADS

    def chunk_copies(bb, cc, slot):
        out = []
        for j in range(pp):
            pg = pt_ref[bb * n_pages + n_pages - (cc + 1) * pp + j]
            for a, (hbm, buf) in enumerate(((kt_hbm, kbuf), (vt_hbm, vbuf), (cc_hbm, cbuf),
                                            (rt_hbm, rbuf), (lt_hbm, lbuf))):
                out.append(pltpu.make_async_copy(hbm.at[layer, pg], buf.at[slot, j], sems.at[a, slot]))
        return out

    @pl.when((b == 0) & (step == 0))
    def _():
        for cp in chunk_copies(0, 0, 0):
            cp.start()

    def attend_first(s, pv):
        m_new = jnp.max(s, axis=-1, keepdims=True)
        p = jnp.exp2(s - m_new)
        l_sc[...] = jnp.sum(p, axis=-1, keepdims=True)
        o_f, o_m = pv(p)
        accf_sc[...] = o_f
        accm_sc[...] = o_m
        m_sc[...] = m_new

    def attend_next(s, pv):
        m_old = m_sc[...]
        m_new = jnp.maximum(m_old, jnp.max(s, axis=-1, keepdims=True))
        p = jnp.exp2(s - m_new)
        alpha = jnp.exp2(m_old - m_new)
        l_sc[...] = alpha * l_sc[...] + jnp.sum(p, axis=-1, keepdims=True)
        o_f, o_m = pv(p)
        accf_sc[...] = alpha[0:nr] * accf_sc[...] + o_f
        accm_sc[...] = alpha[nr:2 * nr] * accm_sc[...] + o_m
        m_sc[...] = m_new

    @pl.when(step == 0)
    def _():
        lane8 = lax.broadcasted_iota(I32, (FOX_HEADS, page), 1)
        c = lfn_ref[...] * LOG2E
        shift = 1
        while shift < dec_seq:
            c = c + jnp.where(lane8 >= shift, pltpu.roll(c, shift, 1), 0.0)
            shift *= 2
        c_key = jnp.concatenate([c] * dec_seq, axis=0)
        c_new = jnp.concatenate([c[:, q:q + 1] for q in range(dec_seq)], axis=0)
        tok = lax.broadcasted_iota(I32, (nr, page), 0) // FOX_HEADS
        lane = lax.broadcasted_iota(I32, (nr, page), 1)
        visible = lane <= tok
        k = kn_ref[...].astype(BF16)
        v = vn_ref[...].astype(BF16)
        cm = cn_ref[...].astype(BF16)
        r = rn_ref[...].astype(BF16)
        s_f = lax.dot_general(qf_ref[...], k, NT_DIMS, preferred_element_type=F32) + (c_new - c_key)
        s_m = (lax.dot_general(ql_ref[...], cm, NT_DIMS, preferred_element_type=F32)
               + lax.dot_general(qr_ref[...], r, NT_DIMS, preferred_element_type=F32))
        s = jnp.concatenate([jnp.where(visible, s_f, MASKED), jnp.where(visible, s_m, MASKED)], axis=0)
        attend_first(s, lambda p: (jnp.dot(p[0:nr].astype(BF16), v, preferred_element_type=F32),
                                   jnp.dot(p[nr:2 * nr].astype(BF16), cm, preferred_element_type=F32)))
        run_sc[...] = jnp.zeros_like(run_sc)
        cnew_sc[...] = c_new

    @pl.when(step >= 1)
    def _():
        cc = step - 1
        n = b * nch + cc
        slot = lax.rem(n, 2)
        for cp in chunk_copies(b, cc, slot):
            cp.wait()

        @pl.when(n + 1 < n_seq * nch)
        def _():
            wrap = cc + 1 == nch
            nb = jnp.where(wrap, b + 1, b)
            nc = jnp.where(wrap, 0, cc + 1)
            for cp in chunk_copies(nb, nc, 1 - slot):
                cp.start()

        lf = lbuf[slot].reshape(pp * FOX_HEADS, page) * LOG2E
        strict = (lax.broadcasted_iota(I32, (page, page), 0)
                  > lax.broadcasted_iota(I32, (page, page), 1)).astype(BF16)
        within = sum(jnp.dot(part, strict, preferred_element_type=F32) for part in _split3_bf16(lf))
        tot = jnp.sum(lf, axis=-1, keepdims=True)
        later = run_sc[...]
        c_new = cnew_sc[...]
        qf = qf_ref[...]
        ql = ql_ref[...]
        qr = qr_ref[...]
        s_f, s_m, vts, cms = [None] * pp, [None] * pp, [], []
        for j in reversed(range(pp)):
            d = within[j * FOX_HEADS:(j + 1) * FOX_HEADS] + later
            later = later + tot[j * FOX_HEADS:(j + 1) * FOX_HEADS]
            kt = kbuf[slot, j].reshape(FOX_KV_WIDTH, page).astype(BF16)
            cm = cbuf[slot, j].astype(BF16)
            rt = rbuf[slot, j].astype(BF16)
            s_f[j] = (jnp.dot(qf, kt, preferred_element_type=F32)
                      + (jnp.concatenate([d] * dec_seq, axis=0) + c_new))
            s_m[j] = (lax.dot_general(ql, cm, NT_DIMS, preferred_element_type=F32)
                      + jnp.dot(qr, rt, preferred_element_type=F32))
        run_sc[...] = later
        s = jnp.concatenate([jnp.concatenate(s_f, axis=1), jnp.concatenate(s_m, axis=1)], axis=0)

        def pv(p):
            o_f = jnp.zeros((nr, FOX_KV_WIDTH), F32)
            o_m = jnp.zeros((nr, MLA_KV_RANK), F32)
            for j in range(pp):
                pj = p[:, j * page:(j + 1) * page].astype(BF16)
                vt = vbuf[slot, j].reshape(FOX_KV_WIDTH, page).astype(BF16)
                o_f = o_f + lax.dot_general(pj[0:nr], vt, NT_DIMS, preferred_element_type=F32)
                o_m = o_m + jnp.dot(pj[nr:2 * nr], cbuf[slot, j].astype(BF16), preferred_element_type=F32)
            return o_f, o_m

        attend_next(s, pv)

    @pl.when(step == nch)
    def _():
        l = l_sc[...]
        of_ref[...] = accf_sc[...] / l[0:nr]
        om_ref[...] = accm_sc[...] / l[nr:2 * nr]


def _sample_attention(layer, page_table, qf, ql, qr, kn, vn, cn, rn, lfn, kt, vt, cc, rt, lt):
    n_seq, n_pages = page_table.shape
    page = cc.shape[2]
    nr = qf.shape[1]
    dec_seq = nr // FOX_HEADS
    pp = PAGES_PER_STEP
    nch = n_pages // pp
    per_seq = lambda *tail: pl.BlockSpec((None,) + tail, lambda b, s, pt: (b,) + (0,) * len(tail))
    hbm = pl.BlockSpec(memory_space=pl.ANY)
    kern = functools.partial(_sample_attn_kernel, layer=layer, n_pages=n_pages, n_seq=n_seq, page=page,
                             dec_seq=dec_seq)
    return pl.pallas_call(
        kern,
        grid_spec=pltpu.PrefetchScalarGridSpec(
            num_scalar_prefetch=1,
            grid=(n_seq, nch + 1),
            in_specs=[per_seq(nr, FOX_KV_WIDTH), per_seq(nr, MLA_KV_RANK), per_seq(nr, MLA_ROPE_DIM),
                      per_seq(page, FOX_KV_WIDTH), per_seq(page, FOX_KV_WIDTH), per_seq(page, MLA_KV_RANK),
                      per_seq(page, MLA_ROPE_DIM), per_seq(FOX_HEADS, page),
                      hbm, hbm, hbm, hbm, hbm],
            out_specs=[per_seq(nr, FOX_KV_WIDTH), per_seq(nr, MLA_KV_RANK)],
            scratch_shapes=[
                pltpu.VMEM((2, pp, FOX_KV_HEADS, HEAD_DIM, page), F32),
                pltpu.VMEM((2, pp, FOX_KV_HEADS, HEAD_DIM, page), F32),
                pltpu.VMEM((2, pp, page, MLA_KV_RANK), F32), pltpu.VMEM((2, pp, MLA_ROPE_DIM, page), F32),
                pltpu.VMEM((2, pp, FOX_HEADS, page), F32), pltpu.SemaphoreType.DMA((5, 2)),
                pltpu.VMEM((2 * nr, 1), F32), pltpu.VMEM((2 * nr, 1), F32),
                pltpu.VMEM((nr, FOX_KV_WIDTH), F32), pltpu.VMEM((nr, MLA_KV_RANK), F32),
                pltpu.VMEM((FOX_HEADS, 1), F32), pltpu.VMEM((nr, 1), F32)],
        ),
        out_shape=[jax.ShapeDtypeStruct((n_seq, nr, FOX_KV_WIDTH), F32),
                   jax.ShapeDtypeStruct((n_seq, nr, MLA_KV_RANK), F32)],
        compiler_params=_params("arbitrary", "arbitrary"),
        name="sample_attn",
    )(page_table.reshape(-1), qf, ql, qr, kn, vn, cn, rn, lfn, kt, vt, cc, rt, lt)


def _merge_kernel(fo_ref, lat_ref, x_ref, gf_ref, gm_ref, wuv_ref, wo_ref, o_ref):
    fn = _rms(fo_ref[...], gf_ref[...])
    mo = jnp.dot(lat_ref[...].astype(BF16), wuv_ref[...], preferred_element_type=F32)
    mn = _rms(mo, gm_ref[...])
    y = (jnp.dot(fn.astype(BF16), wo_ref[0:FOX_WIDTH, :], preferred_element_type=F32)
         + jnp.dot(mn.astype(BF16), wo_ref[FOX_WIDTH:FOX_WIDTH + MLA_WIDTH, :], preferred_element_type=F32))
    o_ref[...] = x_ref[...] + y


def _merge(fo, lat, x, gf, gm, wuv, wo, tm):
    t = x.shape[0]
    row = lambda n: pl.BlockSpec((tm, n), lambda i: (i, 0))
    full = lambda a: pl.BlockSpec(a.shape, lambda i: (0,) * a.ndim)
    return pl.pallas_call(
        _merge_kernel,
        grid=(t // tm,),
        in_specs=[row(FOX_WIDTH), row(MLA_HEADS * MLA_KV_RANK), row(D_MODEL), full(gf), full(gm),
                  full(wuv), full(wo)],
        out_specs=row(D_MODEL),
        out_shape=jax.ShapeDtypeStruct((t, D_MODEL), F32),
        compiler_params=_params("parallel"),
        name="head_merge",
    )(fo, lat, x, gf, gm, wuv, wo)


def _swiglu_chunk(h, wg, wu, wd):
    gate = jnp.dot(h, wg, preferred_element_type=F32)
    up = jnp.dot(h, wu, preferred_element_type=F32)
    a = gate * (1.0 / (1.0 + jnp.exp(-gate))) * up
    return jnp.dot(a.astype(BF16), wd, preferred_element_type=F32)


def _ffn_kernel(x_ref, g_ref, wg_ref, wu_ref, wd_ref, o_ref, h_sc, acc_sc):
    f = pl.program_id(1)

    @pl.when(f == 0)
    def _():
        h_sc[...] = _rms(x_ref[...], g_ref[...]).astype(BF16)
        acc_sc[...] = jnp.zeros_like(acc_sc)

    acc_sc[...] += _swiglu_chunk(h_sc[...], wg_ref[...], wu_ref[...], wd_ref[...])

    @pl.when(f == pl.num_programs(1) - 1)
    def _():
        o_ref[...] = x_ref[...] + acc_sc[...]


def _ffn_dense(x, g, wg, wu, wd, tf, tm):
    t = x.shape[0]
    nf = wg.shape[1] // tf
    return pl.pallas_call(
        _ffn_kernel,
        grid=(t // tm, nf),
        in_specs=[pl.BlockSpec((tm, D_MODEL), lambda i, f: (i, 0)),
                  pl.BlockSpec((1, D_MODEL), lambda i, f: (0, 0)),
                  pl.BlockSpec((D_MODEL, tf), lambda i, f: (0, f)),
                  pl.BlockSpec((D_MODEL, tf), lambda i, f: (0, f)),
                  pl.BlockSpec((tf, D_MODEL), lambda i, f: (f, 0))],
        out_specs=pl.BlockSpec((tm, D_MODEL), lambda i, f: (i, 0)),
        out_shape=jax.ShapeDtypeStruct((t, D_MODEL), F32),
        scratch_shapes=[pltpu.VMEM((tm, D_MODEL), BF16), pltpu.VMEM((tm, D_MODEL), F32)],
        compiler_params=_params("parallel", "arbitrary"),
        name="ffn_dense",
    )(x, g, wg, wu, wd)


def _router_kernel(x_ref, g_ref, wr_ref, idx_ref, w_ref):
    h = _rms(x_ref[...], g_ref[...])
    logits = jnp.dot(h, wr_ref[...], precision=lax.Precision.HIGHEST, preferred_element_type=F32)
    lane = lax.broadcasted_iota(I32, logits.shape, 1)
    v1 = jnp.max(logits, axis=-1, keepdims=True)
    i1 = jnp.min(jnp.where(logits == v1, lane, N_EXPERTS), axis=-1, keepdims=True)
    rest = jnp.where(lane == i1, -jnp.inf, logits)
    v2 = jnp.max(rest, axis=-1, keepdims=True)
    i2 = jnp.min(jnp.where(rest == v2, lane, N_EXPERTS), axis=-1, keepdims=True)
    e2 = jnp.exp(v2 - v1)
    first = lax.broadcasted_iota(I32, idx_ref.shape, 1) == 0
    idx_ref[...] = jnp.where(first, i1, i2)
    w_ref[...] = jnp.where(first, 1.0 / (1.0 + e2), e2 / (1.0 + e2))


def _router(x, g, wr, tm):
    t = x.shape[0]
    return pl.pallas_call(
        _router_kernel,
        grid=(t // tm,),
        in_specs=[pl.BlockSpec((tm, D_MODEL), lambda i: (i, 0)),
                  pl.BlockSpec((1, D_MODEL), lambda i: (0, 0)),
                  pl.BlockSpec((D_MODEL, N_EXPERTS), lambda i: (0, 0))],
        out_specs=[pl.BlockSpec((tm, TOP_K), lambda i: (i, 0)), pl.BlockSpec((tm, TOP_K), lambda i: (i, 0))],
        out_shape=[jax.ShapeDtypeStruct((t, TOP_K), I32), jax.ShapeDtypeStruct((t, TOP_K), F32)],
        compiler_params=_params("parallel"),
        name="moe_router",
    )(x, g, wr)


MOE_ROW_TILE = 512
CHUNKS = D_MODEL // LANES


def _route_tables(idx, w, tmg):
    t = idx.shape[0]
    na = TOP_K * t
    a_exp = idx.reshape(na)
    onehot = (a_exp[:, None] == jnp.arange(N_EXPERTS, dtype=I32)[None, :]).astype(I32)
    csum = jnp.cumsum(onehot, axis=0)
    rank = jnp.take_along_axis(csum, a_exp[:, None], axis=1)[:, 0] - 1
    counts = csum[-1]
    tiles_per = (counts + tmg - 1) // tmg
    tile_end = jnp.cumsum(tiles_per)
    group_start = (tile_end - tiles_per) * tmg
    slot = group_start[a_exp] + rank
    nt = -(-na // tmg) + N_EXPERTS
    nslot = nt * tmg
    slot_a = jnp.full((nslot,), -1, I32).at[slot].set(jnp.arange(na, dtype=I32))
    real = slot_a >= 0
    a = jnp.maximum(slot_a, 0)
    slot_token = a // TOP_K
    slot_dst = (a % TOP_K) * t + slot_token
    slot_w = jnp.where(real, w.reshape(na)[a], 0.0)
    n_used = tile_end[-1]
    tile_ids = jnp.arange(nt, dtype=I32)
    tile_expert = jnp.minimum(jnp.searchsorted(tile_end, jnp.minimum(tile_ids, n_used - 1), side='right'),
                              N_EXPERTS - 1).astype(I32)
    tile_valid = (tile_ids < n_used).astype(I32)
    tile_count = jnp.sum(real.reshape(nt, tmg), axis=1).astype(I32)
    return (tile_expert, tile_valid, tile_count, slot_token.reshape(nt, 1, tmg), slot_dst.reshape(nt, 1, tmg),
            slot_w.reshape(nslot, 1))


def _moe_gmm_kernel(te_ref, tv_ref, tc_ref, tok_ref, ntok_ref, dst_ref, x_hbm, g_ref, w_ref, wg_ref, wu_ref, wd_ref,
                    y_hbm, xbuf, ybuf, h_sc, acc_sc, gsem, ssem, *, tmg):
    i = pl.program_id(0)
    f = pl.program_id(1)
    nt = pl.num_programs(0)
    nf = pl.num_programs(1)
    slot = lax.rem(i, 2)

    def start_gather(tref, sl):
        def body(r, c):
            pltpu.make_async_copy(x_hbm.at[tref[0, r]], xbuf.at[sl, pl.ds(r * CHUNKS, CHUNKS)], gsem.at[sl]).start()
            return c
        lax.fori_loop(0, tmg, body, 0, unroll=8)

    def wait_gather(sl):
        pltpu.make_async_copy(xbuf.at[sl], xbuf.at[sl], gsem.at[sl]).wait()

    def scatter(n, start):
        def body(r, c):
            d = dst_ref[0, r] if start else 0
            cp = pltpu.make_async_copy(ybuf.at[pl.ds(r * CHUNKS, CHUNKS)], y_hbm.at[d], ssem.at[0])
            cp.start() if start else cp.wait()
            return c
        lax.fori_loop(0, n, body, 0)

    start_scatter = lambda n: scatter(n, True)
    wait_scatter = lambda n: scatter(n, False)

    @pl.when(f == 0)
    def _():
        @pl.when(i == 0)
        def _():
            start_gather(tok_ref, 0)

        wait_gather(slot)

        @pl.when(i + 1 < nt)
        def _():
            start_gather(ntok_ref, 1 - slot)

    valid = tv_ref[i] == 1

    @pl.when(valid & (f == 0))
    def _():
        xs = [xbuf[slot, pl.ds(c, tmg, stride=CHUNKS), :] for c in range(CHUNKS)]
        ss = sum(jnp.sum(xc * xc, axis=-1, keepdims=True) for xc in xs)
        rs = lax.rsqrt(ss * (1.0 / D_MODEL) + NORM_EPS)
        for c in range(CHUNKS):
            h_sc[:, c * LANES:(c + 1) * LANES] = (xs[c] * rs * g_ref[:, c * LANES:(c + 1) * LANES]).astype(BF16)
        acc_sc[...] = jnp.zeros_like(acc_sc)

    @pl.when(valid)
    def _():
        acc_sc[...] += _swiglu_chunk(h_sc[...], wg_ref[...], wu_ref[...], wd_ref[...])

    @pl.when(valid & (f == nf - 1))
    def _():
        @pl.when(i > 0)
        def _():
            wait_scatter(tc_ref[jnp.maximum(i - 1, 0)])

        y = acc_sc[...] * w_ref[...]
        for c in range(CHUNKS):
            ybuf[pl.ds(c, tmg, stride=CHUNKS), :] = y[:, c * LANES:(c + 1) * LANES]
        start_scatter(tc_ref[i])

        last_valid = jnp.where(i + 1 < nt, tv_ref[jnp.minimum(i + 1, nt - 1)], 0) == 0

        @pl.when(last_valid)
        def _():
            wait_scatter(tc_ref[i])


def _moe_gmm(x3, g, tables, wg, wu, wd, tf, tmg):
    tile_expert, tile_valid, tile_count, slot_token, slot_dst, slot_w = tables
    t = x3.shape[0]
    nt = slot_token.shape[0]
    ne, _, ff = wg.shape
    nf = ff // tf
    smem_tile = lambda fn: pl.BlockSpec((None, 1, tmg), fn, memory_space=pltpu.SMEM)
    return pl.pallas_call(
        functools.partial(_moe_gmm_kernel, tmg=tmg),
        grid_spec=pltpu.PrefetchScalarGridSpec(
            num_scalar_prefetch=3,
            grid=(nt, nf),
            in_specs=[smem_tile(lambda i, f, te, tv, tc: (i, 0, 0)),
                      smem_tile(lambda i, f, te, tv, tc: (jnp.minimum(i + 1, nt - 1), 0, 0)),
                      smem_tile(lambda i, f, te, tv, tc: (i, 0, 0)),
                      pl.BlockSpec(memory_space=pl.ANY),
                      pl.BlockSpec((1, D_MODEL), lambda i, f, te, tv, tc: (0, 0)),
                      pl.BlockSpec((tmg, 1), lambda i, f, te, tv, tc: (i, 0)),
                      pl.BlockSpec((None, D_MODEL, tf), lambda i, f, te, tv, tc: (te[i], 0, f)),
                      pl.BlockSpec((None, D_MODEL, tf), lambda i, f, te, tv, tc: (te[i], 0, f)),
                      pl.BlockSpec((None, tf, D_MODEL), lambda i, f, te, tv, tc: (te[i], f, 0))],
            out_specs=pl.BlockSpec(memory_space=pl.ANY),
            scratch_shapes=[pltpu.VMEM((2, tmg * CHUNKS, LANES), F32), pltpu.VMEM((tmg * CHUNKS, LANES), F32),
                            pltpu.VMEM((tmg, D_MODEL), BF16), pltpu.VMEM((tmg, D_MODEL), F32),
                            pltpu.SemaphoreType.DMA((2,)), pltpu.SemaphoreType.DMA((1,))],
        ),
        out_shape=jax.ShapeDtypeStruct((TOP_K * t, CHUNKS, LANES), F32),
        compiler_params=_params("arbitrary", "arbitrary"),
        name="moe_experts",
    )(tile_expert, tile_valid, tile_count, slot_token, slot_token, slot_dst, x3, g, slot_w, wg, wu, wd)


def _moe_combine_kernel(x_ref, y0_ref, y1_ref, o_ref):
    o_ref[...] = (x_ref[...] + y0_ref[...]) + y1_ref[...]


def _moe_combine(x3, y3, tm):
    t = x3.shape[0]
    nb = t // tm
    blk = lambda fn: pl.BlockSpec((tm, CHUNKS, LANES), fn)
    return pl.pallas_call(
        _moe_combine_kernel,
        grid=(nb,),
        in_specs=[blk(lambda i: (i, 0, 0)), blk(lambda i: (i, 0, 0)), blk(lambda i: (i + nb, 0, 0))],
        out_specs=blk(lambda i: (i, 0, 0)),
        out_shape=jax.ShapeDtypeStruct(x3.shape, F32),
        compiler_params=_params("parallel"),
        name="moe_combine",
    )(x3, y3, y3)


def _final_norm_kernel(x_ref, g_ref, o_ref):
    o_ref[...] = _rms(x_ref[...], g_ref[...])


def _final_norm(x, g, tm):
    t = x.shape[0]
    return pl.pallas_call(
        _final_norm_kernel,
        grid=(t // tm,),
        in_specs=[pl.BlockSpec((tm, D_MODEL), lambda i: (i, 0)), pl.BlockSpec((1, D_MODEL), lambda i: (0, 0))],
        out_specs=pl.BlockSpec((tm, D_MODEL), lambda i: (i, 0)),
        out_shape=jax.ShapeDtypeStruct((t, D_MODEL), F32),
        compiler_params=_params("parallel"),
        name="final_norm",
    )(x, g)


def _rotate_half_cols(w):
    half = w.shape[-1] // 2
    return jnp.concatenate([-w[..., half:], w[..., :half]], axis=-1)


def _prep_w_in(w):
    ends = [FOX_WIDTH, FOX_WIDTH + FOX_KV_WIDTH, FOX_WIDTH + 2 * FOX_KV_WIDTH]
    ends.append(ends[-1] + FOX_HEADS)
    ends.append(ends[-1] + MLA_Q_RANK)
    ends.append(ends[-1] + MLA_KV_RANK)
    fq, fk, fv, fz, cq, ckv, kr = jnp.split(w, ends, axis=-1)
    pad = jnp.zeros((w.shape[0], LANES - 2 * MLA_ROPE_DIM - FOX_HEADS), w.dtype)
    return jnp.concatenate([fq * (FOX_SCALE * LOG2E), fk, fv, cq, ckv, kr, _rotate_half_cols(kr), fz, pad],
                           axis=-1).astype(BF16)


def _prep_w_uq(w_uq):
    nope = w_uq[:, :, :MLA_NOPE_DIM]
    rope = w_uq[:, :, MLA_NOPE_DIM:]
    padto = lambda a: jnp.pad(a, ((0, 0), (0, 0), (0, LANES - a.shape[-1])))
    w = jnp.concatenate([padto(nope), padto(rope), padto(_rotate_half_cols(rope))], axis=-1)
    return jnp.transpose(w, (1, 0, 2)).astype(BF16)


def _prep_w_uk(w_uk):
    w = jnp.transpose(w_uk, (1, 2, 0))
    return jnp.pad(w, ((0, 0), (0, LANES - w.shape[1]), (0, 0))).astype(BF16)


def _prep_w_uv(w_uv):
    c, h, v = w_uv.shape
    bd = jnp.einsum('chv,hg->hcgv', w_uv, jnp.eye(h, dtype=w_uv.dtype))
    return bd.reshape(h * c, h * v).astype(BF16)


def _rope_tables(pos):
    half = MLA_ROPE_DIM // 2
    inv_freq = jnp.power(ROPE_THETA, -jnp.arange(half, dtype=F32) / half)
    ang = pos.astype(F32)[:, None] * inv_freq[None, :]
    pad = lambda a: jnp.pad(jnp.concatenate([a, a], axis=-1), ((0, 0), (0, LANES - MLA_ROPE_DIM)))
    return pad(jnp.cos(ang)), pad(jnp.sin(ang))


def kernel(x_prompt, x_sample, cache_fox_k, cache_fox_v, cache_fox_logf, cache_mla_ckv, cache_mla_krope, page_table, w_in, b_f, g_attn, g_cq, g_ckv, w_uq, w_uk, w_uv, g_fox_out, g_mla_out, w_o, g_ffn, w_gate_dense, w_up_dense, w_down_dense, w_router, w_gate_exp, w_up_exp, w_down_exp, g_final):
    batch, seq, _ = x_prompt.shape
    n_seq, dec_seq, _ = x_sample.shape
    depth, n_pool, page = cache_fox_k.shape[:3]
    n_pages = page_table.shape[1]
    past_len = n_pages * page
    tp = batch * seq
    ts = n_seq * dec_seq
    t_all = tp + ts
    tm = _largest_tile(TOKEN_TILE, tp, ts)
    tk = _largest_tile(512, seq)
    tmg = _largest_tile(MOE_ROW_TILE, t_all)
    nr = dec_seq * FOX_HEADS
    assert n_pages % PAGES_PER_STEP == 0 and dec_seq <= page

    x = jnp.concatenate([x_prompt.reshape(tp, D_MODEL), x_sample.reshape(ts, D_MODEL)], axis=0)
    pos = jnp.concatenate([jnp.tile(jnp.arange(seq, dtype=I32), batch),
                           past_len + jnp.tile(jnp.arange(dec_seq, dtype=I32), n_seq)])
    cos, sin = _rope_tables(pos)

    kt = jnp.transpose(cache_fox_k, (0, 1, 3, 4, 2))
    vt = jnp.transpose(cache_fox_v, (0, 1, 3, 4, 2))
    rt = jnp.transpose(cache_mla_krope, (0, 1, 3, 2))
    lt = jnp.transpose(cache_fox_logf, (0, 1, 3, 2))

    row = lambda a: a.reshape(1, -1)
    outs = [[] for _ in range(10)]
    eye_kv = jnp.eye(FOX_KV_HEADS, dtype=BF16)
    pad_rows = lambda a: jnp.pad(a.reshape(n_seq, dec_seq, -1), ((0, 0), (0, page - dec_seq), (0, 0)))

    for l in range(depth):
        fq, fk, fv, fkb, fvb, cq, ckv, kr, kcat, lf = _proj(
            x, row(g_attn[l]), _prep_w_in(w_in[l]), row(g_ckv[l]), row(b_f[l]), cos, sin, tm)
        qcat = _mla_q(cq, row(g_cq[l]), _prep_w_uq(w_uq[l]), _prep_w_uk(w_uk[l]), cos, sin, tm)

        c = _cumsum(lf[:tp].reshape(batch, seq, FOX_HEADS), tk)
        c_t = jnp.transpose(c, (0, 2, 1))
        fo_p = _fox_prompt_attention(fq, fkb, fvb, c_t[:, :, :, None], c_t[:, :, None, :], batch, seq,
                                     _largest_tile(FOX_Q_TILE, seq), tk)
        lat_p = _mla_prompt_attention(qcat, kcat, batch, seq, _largest_tile(MLA_Q_TILE, seq), tk)

        q5 = fq[tp:].reshape(n_seq, dec_seq, FOX_KV_HEADS, FOX_GROUP, HEAD_DIM)
        qf = jnp.einsum('bqkgd,kj->bqkgjd', q5, eye_kv).reshape(n_seq, nr, FOX_KV_WIDTH)
        qm = jnp.transpose(qcat[:, tp:].reshape(MLA_HEADS, n_seq, dec_seq, MLA_QK_DIM),
                           (1, 2, 0, 3)).reshape(n_seq, nr, MLA_QK_DIM)
        lfn = jnp.transpose(pad_rows(lf[tp:]), (0, 2, 1))
        of, om = _sample_attention(l, page_table, qf, qm[:, :, :MLA_KV_RANK], qm[:, :, MLA_KV_RANK:],
                                   pad_rows(fk[tp:]), pad_rows(fv[tp:]), pad_rows(ckv[tp:]), pad_rows(kr[tp:]),
                                   lfn, kt, vt, cache_mla_ckv, rt, lt)
        of6 = of.reshape(n_seq, dec_seq, FOX_KV_HEADS, FOX_GROUP, FOX_KV_HEADS, HEAD_DIM)
        fo_s = jnp.einsum('bqkgjd,kj->bqkgd', of6, jnp.eye(FOX_KV_HEADS, dtype=F32)).reshape(ts, FOX_WIDTH)
        lat_s = om.reshape(ts, MLA_HEADS * MLA_KV_RANK)

        x = _merge(jnp.concatenate([fo_p, fo_s], axis=0), jnp.concatenate([lat_p, lat_s], axis=0), x,
                   row(g_fox_out[l]), row(g_mla_out[l]), _prep_w_uv(w_uv[l]), w_o[l].astype(BF16), tm)

        i = l // 2
        if l % 2 == 0:
            x = _ffn_dense(x, row(g_ffn[l]), w_gate_dense[i].astype(BF16), w_up_dense[i].astype(BF16),
                           w_down_dense[i].astype(BF16), _ff_tile(w_gate_dense.shape[-1]), tm)
        else:
            idx, wts = _router(x, row(g_ffn[l]), w_router[i], tm)
            x3 = x.reshape(t_all, CHUNKS, LANES)
            y3 = _moe_gmm(x3, row(g_ffn[l]), _route_tables(idx, wts, tmg), w_gate_exp[i].astype(BF16),
                          w_up_exp[i].astype(BF16), w_down_exp[i].astype(BF16),
                          _ff_tile(w_gate_exp.shape[-1]), tmg)
            x = _moe_combine(x3, y3, tmg).reshape(t_all, D_MODEL)

        for dst, a in zip(outs, (fk[:tp], fv[:tp], lf[:tp], ckv[:tp], kr[:tp],
                                 fk[tp:], fv[tp:], lf[tp:], ckv[tp:], kr[tp:])):
            dst.append(a)

    y = _final_norm(x, row(g_final), tm)
    shapes = [(batch, seq, FOX_KV_HEADS, HEAD_DIM), (batch, seq, FOX_KV_HEADS, HEAD_DIM), (batch, seq, FOX_HEADS),
              (batch, seq, MLA_KV_RANK), (batch, seq, MLA_ROPE_DIM),
              (n_seq, dec_seq, FOX_KV_HEADS, HEAD_DIM), (n_seq, dec_seq, FOX_KV_HEADS, HEAD_DIM),
              (n_seq, dec_seq, FOX_HEADS), (n_seq, dec_seq, MLA_KV_RANK), (n_seq, dec_seq, MLA_ROPE_DIM)]
    caches = tuple(jnp.stack(o).reshape((depth,) + s) for o, s in zip(outs, shapes))
    return (y[:tp].reshape(batch, seq, D_MODEL), y[tp:].reshape(n_seq, dec_seq, D_MODEL)) + caches
```

```python
import functools
import math

import jax
import jax.numpy as jnp
import numpy as np
from jax import lax
from jax.experimental import pallas as pl
from jax.experimental.pallas import tpu as pltpu

F32 = jnp.float32
BF16 = jnp.bfloat16
I32 = jnp.int32

D_MODEL = 1024
HEAD_DIM = 64
FOX_HEADS = 8
FOX_KV_HEADS = 4
FOX_GROUP = FOX_HEADS // FOX_KV_HEADS
FOX_WIDTH = FOX_HEADS * HEAD_DIM
FOX_KV_WIDTH = FOX_KV_HEADS * HEAD_DIM
MLA_HEADS = 8
MLA_NOPE_DIM = 64
MLA_ROPE_DIM = 32
MLA_V_DIM = 64
MLA_Q_RANK = 256
MLA_KV_RANK = 128
MLA_WIDTH = MLA_HEADS * MLA_V_DIM
MLA_QK_DIM = MLA_KV_RANK + MLA_ROPE_DIM
N_EXPERTS = 8
TOP_K = 2
ROPE_THETA = 10000.0
NORM_EPS = 1e-6
LOG2E = math.log2(math.e)
FOX_SCALE = HEAD_DIM ** -0.5
MLA_SCALE = (MLA_NOPE_DIM + MLA_ROPE_DIM) ** -0.5
MASKED = -1e30

LANES = 128
SUBLANES = 8
VMEM_LIMIT_BYTES = 48 * 1024 * 1024

C_FQ = 0
C_FK = C_FQ + FOX_WIDTH
C_FV = C_FK + FOX_KV_WIDTH
C_CQ = C_FV + FOX_KV_WIDTH
C_CKV = C_CQ + MLA_Q_RANK
C_MISC = C_CKV + MLA_KV_RANK
PROJ_COLS = C_MISC + LANES
M_KRR = MLA_ROPE_DIM
M_FZ = 2 * MLA_ROPE_DIM

TOKEN_TILE = 512
NT_DIMS = (((1,), (1,)), ((), ()))


def _largest_tile(cap, *sizes):
    t = cap
    while any(s % t for s in sizes):
        t //= 2
    return t


FF_TILE_CAP = 1536


def _ff_tile(ff):
    return max(c for c in range(LANES, FF_TILE_CAP + 1, LANES) if ff % c == 0)


def _rms(x, g):
    return x * lax.rsqrt(jnp.mean(x * x, axis=-1, keepdims=True) + NORM_EPS) * g


def _params(*sem):
    return pltpu.CompilerParams(dimension_semantics=sem, vmem_limit_bytes=VMEM_LIMIT_BYTES)


def _proj_kernel(x_ref, g_ref, w_ref, gckv_ref, bf_ref, cos_ref, sin_ref,
                 fq_ref, fk_ref, fv_ref, fkb_ref, fvb_ref, cq_ref, ckv_ref, kr_ref, kcat_ref, lf_ref):
    xn = _rms(x_ref[...], g_ref[...])
    p = jnp.dot(xn.astype(BF16), w_ref[...], preferred_element_type=F32)
    fq_ref[...] = p[:, C_FQ:C_FK].astype(BF16)
    fk = p[:, C_FK:C_FV]
    fv = p[:, C_FV:C_CQ]
    fk_ref[...] = fk
    fv_ref[...] = fv
    fkb_ref[...] = fk.astype(BF16)
    fvb_ref[...] = fv.astype(BF16)
    cq_ref[...] = p[:, C_CQ:C_CKV]
    ckv = _rms(p[:, C_CKV:C_MISC], gckv_ref[...])
    ckv_ref[...] = ckv
    misc = p[:, C_MISC:PROJ_COLS]
    rot = misc * cos_ref[...] + pltpu.roll(misc, LANES - M_KRR, 1) * sin_ref[...]
    kr = rot[:, 0:MLA_ROPE_DIM]
    kr_ref[...] = kr
    kcat_ref[:, 0:MLA_KV_RANK] = ckv.astype(BF16)
    kcat_ref[:, MLA_KV_RANK:MLA_QK_DIM] = kr.astype(BF16)
    z = pltpu.roll(misc, LANES - M_FZ, 1)[:, 0:FOX_HEADS] + bf_ref[...]
    lf_ref[...] = jnp.minimum(z, 0.0) - jnp.log1p(jnp.exp(-jnp.abs(z)))


def _proj(x, g, w, gckv, bf, cos, sin, tm):
    t = x.shape[0]
    row = lambda n: pl.BlockSpec((tm, n), lambda i: (i, 0))
    full = lambda a: pl.BlockSpec(a.shape, lambda i: (0,) * a.ndim)
    outs = [(FOX_WIDTH, BF16), (FOX_KV_WIDTH, F32), (FOX_KV_WIDTH, F32), (FOX_KV_WIDTH, BF16),
            (FOX_KV_WIDTH, BF16), (MLA_Q_RANK, F32), (MLA_KV_RANK, F32), (MLA_ROPE_DIM, F32),
            (MLA_QK_DIM, BF16), (FOX_HEADS, F32)]
    return pl.pallas_call(
        _proj_kernel,
        grid=(t // tm,),
        in_specs=[row(D_MODEL), full(g), full(w), full(gckv), full(bf), row(LANES), row(LANES)],
        out_specs=[row(n) for n, _ in outs],
        out_shape=[jax.ShapeDtypeStruct((t, n), d) for n, d in outs],
        compiler_params=_params("parallel"),
        name="in_proj",
    )(x, g, w, gckv, bf, cos, sin)


def _mlaq_kernel(cq_ref, g_ref, wq_ref, wuk_ref, cos_ref, sin_ref, o_ref):
    cqn = _rms(cq_ref[...], g_ref[...]).astype(BF16)
    scale = MLA_SCALE * LOG2E
    for h in range(MLA_HEADS):
        qa = jnp.dot(cqn, wq_ref[h], preferred_element_type=F32)
        lat = jnp.dot(qa[:, 0:LANES].astype(BF16), wuk_ref[h], preferred_element_type=F32)
        rot = qa[:, LANES:2 * LANES] * cos_ref[...] + qa[:, 2 * LANES:3 * LANES] * sin_ref[...]
        o_ref[h, :, 0:MLA_KV_RANK] = (lat * scale).astype(BF16)
        o_ref[h, :, MLA_KV_RANK:MLA_QK_DIM] = (rot[:, 0:MLA_ROPE_DIM] * scale).astype(BF16)


def _mla_q(cq, g, wq, wuk, cos, sin, tm):
    t = cq.shape[0]
    row = lambda n: pl.BlockSpec((tm, n), lambda i: (i, 0))
    full = lambda a: pl.BlockSpec(a.shape, lambda i: (0,) * a.ndim)
    return pl.pallas_call(
        _mlaq_kernel,
        grid=(t // tm,),
        in_specs=[row(MLA_Q_RANK), full(g), full(wq), full(wuk), row(LANES), row(LANES)],
        out_specs=pl.BlockSpec((MLA_HEADS, tm, MLA_QK_DIM), lambda i: (0, i, 0)),
        out_shape=jax.ShapeDtypeStruct((MLA_HEADS, t, MLA_QK_DIM), BF16),
        compiler_params=_params("parallel"),
        name="mla_q",
    )(cq, g, wq, wuk, cos, sin)


def _cumsum_kernel(lf_ref, o_ref, carry_sc):
    @pl.when(pl.program_id(1) == 0)
    def _():
        carry_sc[...] = jnp.zeros_like(carry_sc)

    n = lf_ref.shape[0]
    tri = (lax.broadcasted_iota(I32, (n, n), 1) <= lax.broadcasted_iota(I32, (n, n), 0)).astype(F32)
    c = jnp.dot(tri, lf_ref[...], precision=lax.Precision.HIGHEST, preferred_element_type=F32) + carry_sc[...]
    o_ref[...] = c * LOG2E
    carry_sc[...] = c[n - 1:n, :]


def _cumsum(lf, tile):
    b, s, h = lf.shape
    spec = pl.BlockSpec((None, tile, h), lambda i, j: (i, j, 0))
    return pl.pallas_call(
        _cumsum_kernel,
        grid=(b, s // tile),
        in_specs=[spec],
        out_specs=spec,
        out_shape=jax.ShapeDtypeStruct(lf.shape, F32),
        scratch_shapes=[pltpu.VMEM((1, h), F32)],
        compiler_params=_params("parallel", "arbitrary"),
        name="logf_cumsum",
    )(lf)


def _last_key_block(qi, tq, tk):
    return ((qi + 1) * tq - 1) // tk


def _triangle(nq, tq, tk):
    pairs = [(i, j) for i in range(nq) for j in range(_last_key_block(i, tq, tk) + 1)]
    return (jnp.asarray(np.array([p[0] for p in pairs], np.int32)),
            jnp.asarray(np.array([p[1] for p in pairs], np.int32)))


def _online_softmax(u, m_old, v, rows_bias=None):
    m_blk = jnp.max(u, axis=-1, keepdims=True)
    if rows_bias is not None:
        m_blk = m_blk + rows_bias
    m_new = jnp.maximum(m_old, m_blk)
    shift = m_new if rows_bias is None else m_new - rows_bias
    p = jnp.exp2(u - shift)
    alpha = jnp.exp2(m_old - m_new)
    return m_new, alpha, jnp.sum(p, axis=-1, keepdims=True), jnp.dot(p.astype(BF16), v, preferred_element_type=F32)


FOX_ROW_SPLIT = 1
MLA_LOOKAHEAD = 1
FOX_Q_TILE = 512
MLA_Q_TILE = 256


def _fox_flash_kernel(qi_ref, ki_ref, q_ref, k_ref, v_ref, cq_ref, ck_ref, o_ref, m_sc, l_sc, acc_sc, *, tq, tk):
    t = pl.program_id(2)
    qi = qi_ref[t]
    ki = ki_ref[t]
    last = _last_key_block(qi, tq, tk)

    @pl.when(ki == 0)
    def _():
        m_sc[...] = jnp.full_like(m_sc, MASKED)
        l_sc[...] = jnp.zeros_like(l_sc)
        acc_sc[...] = jnp.zeros_like(acc_sc)

    def step(diagonal):
        tr = tq // FOX_ROW_SPLIT
        if diagonal:
            qpos = qi * tq + lax.broadcasted_iota(I32, (tq, tk), 0)
            kpos = ki * tk + lax.broadcasted_iota(I32, (tq, tk), 1)
            visible = kpos <= qpos
        m_all = m_sc[...]
        l_all = l_sc[...]
        acc_all = acc_sc[...]
        res = []
        nchain = 2 * FOX_GROUP * FOX_ROW_SPLIT
        kv = lambda ref, h: ref[:, (h // FOX_GROUP) * HEAD_DIM:(h // FOX_GROUP + 1) * HEAD_DIM]

        def scores(c):
            h, part = divmod(c, FOX_ROW_SPLIT)
            return lax.dot_general(q_ref[part * tr:(part + 1) * tr, h * HEAD_DIM:(h + 1) * HEAD_DIM],
                                   kv(k_ref, h), NT_DIMS, preferred_element_type=F32)

        s_next = scores(0)
        for c in range(nchain):
            h, part = divmod(c, FOX_ROW_SPLIT)
            s = s_next
            if c + 1 < nchain:
                s_next = scores(c + 1)
            u = s - ck_ref[h]
            if diagonal:
                u = jnp.where(visible[part * tr:(part + 1) * tr], u, MASKED)
            m_new, alpha, psum, pv = _online_softmax(u, m_all[c], kv(v_ref, h),
                                                     cq_ref[h, part * tr:(part + 1) * tr])
            res.append((m_new, alpha * l_all[c] + psum, alpha * acc_all[c] + pv))
        m_sc[...] = jnp.stack([r[0] for r in res])
        l_sc[...] = jnp.stack([r[1] for r in res])
        acc_sc[...] = jnp.stack([r[2] for r in res])

    @pl.when(ki < last)
    def _():
        step(False)

    @pl.when(ki == last)
    def _():
        step(True)
        tr = tq // FOX_ROW_SPLIT
        for c in range(2 * FOX_GROUP * FOX_ROW_SPLIT):
            h, part = divmod(c, FOX_ROW_SPLIT)
            o_ref[part * tr:(part + 1) * tr, h * HEAD_DIM:(h + 1) * HEAD_DIM] = acc_sc[c] / l_sc[c]


def _fox_prompt_attention(q, k, v, cq, ck, batch, seq, tq, tk):
    assert tk % tq == 0
    nq, nk = seq // tq, seq // tk
    qi_tab, ki_tab = _triangle(nq, tq, tk)
    nhead = 2 * FOX_GROUP
    nchain, tr = nhead * FOX_ROW_SPLIT, tq // FOX_ROW_SPLIT
    qmap = lambda b, hb, t, qi, ki: (b * nq + qi[t], hb)
    kmap = lambda b, hb, t, qi, ki: (b * nk + ki[t], hb)
    return pl.pallas_call(
        functools.partial(_fox_flash_kernel, tq=tq, tk=tk),
        grid_spec=pltpu.PrefetchScalarGridSpec(
            num_scalar_prefetch=2,
            grid=(batch, FOX_KV_HEADS // 2, qi_tab.shape[0]),
            in_specs=[
                pl.BlockSpec((tq, nhead * HEAD_DIM), qmap),
                pl.BlockSpec((tk, 2 * HEAD_DIM), kmap),
                pl.BlockSpec((tk, 2 * HEAD_DIM), kmap),
                pl.BlockSpec((None, nhead, tq, 1), lambda b, hb, t, qi, ki: (b, hb, qi[t], 0)),
                pl.BlockSpec((None, nhead, 1, tk), lambda b, hb, t, qi, ki: (b, hb, 0, ki[t])),
            ],
            out_specs=pl.BlockSpec((tq, nhead * HEAD_DIM), qmap),
            scratch_shapes=[pltpu.VMEM((nchain, tr, 1), F32), pltpu.VMEM((nchain, tr, 1), F32),
                            pltpu.VMEM((nchain, tr, HEAD_DIM), F32)],
        ),
        out_shape=jax.ShapeDtypeStruct((batch * seq, FOX_WIDTH), F32),
        compiler_params=_params("parallel", "parallel", "arbitrary"),
        name="fox_prompt_attn",
    )(qi_tab, ki_tab, q, k, v, cq, ck)


def _mla_flash_kernel(qi_ref, ki_ref, q_ref, k_ref, o_ref, m_sc, l_sc, acc_sc, *, tq, tk):
    t = pl.program_id(1)
    qi = qi_ref[t]
    ki = ki_ref[t]
    last = _last_key_block(qi, tq, tk)

    @pl.when(ki == 0)
    def _():
        m_sc[...] = jnp.full_like(m_sc, MASKED)
        l_sc[...] = jnp.zeros_like(l_sc)
        acc_sc[...] = jnp.zeros_like(acc_sc)

    def step(diagonal):
        kc = k_ref[...]
        vc = kc[:, 0:MLA_KV_RANK]
        if diagonal:
            qpos = qi * tq + lax.broadcasted_iota(I32, (tq, tk), 0)
            kpos = ki * tk + lax.broadcasted_iota(I32, (tq, tk), 1)
            visible = kpos <= qpos
        m_all = m_sc[...]
        l_all = l_sc[...]
        acc_all = acc_sc[...]
        res = []
        scores = lambda h: lax.dot_general(q_ref[h], kc, NT_DIMS, preferred_element_type=F32)
        ahead = [scores(h) for h in range(MLA_LOOKAHEAD)]
        for h in range(MLA_HEADS):
            s = ahead.pop(0)
            if h + MLA_LOOKAHEAD < MLA_HEADS:
                ahead.append(scores(h + MLA_LOOKAHEAD))
            if diagonal:
                s = jnp.where(visible, s, MASKED)
            m_new, alpha, psum, pv = _online_softmax(s, m_all[h], vc)
            res.append((m_new, alpha * l_all[h] + psum, alpha * acc_all[h] + pv))
        m_sc[...] = jnp.stack([r[0] for r in res])
        l_sc[...] = jnp.stack([r[1] for r in res])
        acc_sc[...] = jnp.stack([r[2] for r in res])

    @pl.when(ki < last)
    def _():
        step(False)

    @pl.when(ki == last)
    def _():
        step(True)
        for h in range(MLA_HEADS):
            o_ref[:, h * MLA_KV_RANK:(h + 1) * MLA_KV_RANK] = acc_sc[h] / l_sc[h]


def _mla_prompt_attention(q, kcat, batch, seq, tq, tk):
    assert tk % tq == 0
    nq, nk = seq // tq, seq // tk
    qi_tab, ki_tab = _triangle(nq, tq, tk)
    return pl.pallas_call(
        functools.partial(_mla_flash_kernel, tq=tq, tk=tk),
        grid_spec=pltpu.PrefetchScalarGridSpec(
            num_scalar_prefetch=2,
            grid=(batch, qi_tab.shape[0]),
            in_specs=[
                pl.BlockSpec((MLA_HEADS, tq, MLA_QK_DIM), lambda b, t, qi, ki: (0, b * nq + qi[t], 0)),
                pl.BlockSpec((tk, MLA_QK_DIM), lambda b, t, qi, ki: (b * nk + ki[t], 0)),
            ],
            out_specs=pl.BlockSpec((tq, MLA_HEADS * MLA_KV_RANK), lambda b, t, qi, ki: (b * nq + qi[t], 0)),
            scratch_shapes=[pltpu.VMEM((MLA_HEADS, tq, 1), F32), pltpu.VMEM((MLA_HEADS, tq, 1), F32),
                            pltpu.VMEM((MLA_HEADS, tq, MLA_KV_RANK), F32)],
        ),
        out_shape=jax.ShapeDtypeStruct((batch * seq, MLA_HEADS * MLA_KV_RANK), F32),
        compiler_params=_params("parallel", "arbitrary"),
        name="mla_prompt_attn",
    )(qi_tab, ki_tab, q, kcat)


PAGES_PER_STEP = 16
PAGE_RING_SLOTS = 3


def _split3_bf16(x):
    hi = x.astype(BF16)
    r1 = x - hi.astype(F32)
    mid = r1.astype(BF16)
    lo = (r1 - mid.astype(F32)).astype(BF16)
    return hi, mid, lo


def _sample_attn_kernel(pt_ref, qf_ref, ql_ref, qr_ref, kn_ref, vn_ref, cn_ref, rn_ref, lfn_ref,
                        kt_hbm, vt_hbm, cc_hbm, rt_hbm, lt_hbm,
                        of_ref, om_ref,
                        kbuf, vbuf, cbuf, rbuf, lbuf, sems, m_sc, l_sc, accf_sc, accm_sc, run_sc, cnew_sc,
                        *, layer, n_pages, n_seq, page, dec_seq):
    b = pl.program_id(0)
    step = pl.program_id(1)
    pp = PAGES_PER_STEP
    nch = n_pages // pp
    nr = dec_seq * FOX_HEADS

    def chunk_copies(bb, cc, slot):
        out = []
        for j in range(pp):
            pg = pt_ref[bb * n_pages + n_pages - (cc + 1) * pp + j]
            for a, (hbm, buf) in enumerate(((kt_hbm, kbuf), (vt_hbm, vbuf), (cc_hbm, cbuf),
                                            (rt_hbm, rbuf), (lt_hbm, lbuf))):
                out.append(pltpu.make_async_copy(hbm.at[layer, pg], buf.at[slot, j], sems.at[a, slot]))
        return out

    def start_chunk(n):
        for cp in chunk_copies(n // nch, n % nch, n % PAGE_RING_SLOTS):
            cp.start()

    @pl.when((b == 0) & (step == 0))
    def _():
        for n0 in range(min(PAGE_RING_SLOTS - 1, n_seq * nch)):
            start_chunk(n0)

    def attend_first(s, pv):
        m_new = jnp.max(s, axis=-1, keepdims=True)
        p = jnp.exp2(s - m_new)
        l_sc[...] = jnp.sum(p, axis=-1, keepdims=True)
        o_f, o_m = pv(p)
        accf_sc[...] = o_f
        accm_sc[...] = o_m
        m_sc[...] = m_new

    def attend_next(s, pv):
        m_old = m_sc[...]
        m_new = jnp.maximum(m_old, jnp.max(s, axis=-1, keepdims=True))
        p = jnp.exp2(s - m_new)
        alpha = jnp.exp2(m_old - m_new)
        l_sc[...] = alpha * l_sc[...] + jnp.sum(p, axis=-1, keepdims=True)
        o_f, o_m = pv(p)
        accf_sc[...] = alpha[0:nr] * accf_sc[...] + o_f
        accm_sc[...] = alpha[nr:2 * nr] * accm_sc[...] + o_m
        m_sc[...] = m_new

    @pl.when(step == 0)
    def _():
        lane8 = lax.broadcasted_iota(I32, (FOX_HEADS, page), 1)
        c = lfn_ref[...] * LOG2E
        shift = 1
        while shift < dec_seq:
            c = c + jnp.where(lane8 >= shift, pltpu.roll(c, shift, 1), 0.0)
            shift *= 2
        c_key = jnp.concatenate([c] * dec_seq, axis=0)
        c_new = jnp.concatenate([c[:, q:q + 1] for q in range(dec_seq)], axis=0)
        tok = lax.broadcasted_iota(I32, (nr, page), 0) // FOX_HEADS
        lane = lax.broadcasted_iota(I32, (nr, page), 1)
        visible = lane <= tok
        k = kn_ref[...].astype(BF16)
        v = vn_ref[...].astype(BF16)
        cm = cn_ref[...].astype(BF16)
        r = rn_ref[...].astype(BF16)
        s_f = lax.dot_general(qf_ref[...], k, NT_DIMS, preferred_element_type=F32) + (c_new - c_key)
        s_m = (lax.dot_general(ql_ref[...], cm, NT_DIMS, preferred_element_type=F32)
               + lax.dot_general(qr_ref[...], r, NT_DIMS, preferred_element_type=F32))
        s = jnp.concatenate([jnp.where(visible, s_f, MASKED), jnp.where(visible, s_m, MASKED)], axis=0)
        attend_first(s, lambda p: (jnp.dot(p[0:nr].astype(BF16), v, preferred_element_type=F32),
                                   jnp.dot(p[nr:2 * nr].astype(BF16), cm, preferred_element_type=F32)))
        run_sc[...] = jnp.zeros_like(run_sc)
        cnew_sc[...] = c_new

    @pl.when(step >= 1)
    def _():
        cc = step - 1
        n = b * nch + cc
        slot = lax.rem(n, PAGE_RING_SLOTS)
        for cp in chunk_copies(b, cc, slot):
            cp.wait()

        @pl.when(n + PAGE_RING_SLOTS - 1 < n_seq * nch)
        def _():
            start_chunk(n + PAGE_RING_SLOTS - 1)

        lf = lbuf[slot].reshape(pp * FOX_HEADS, page) * LOG2E
        strict = (lax.broadcasted_iota(I32, (page, page), 0)
                  > lax.broadcasted_iota(I32, (page, page), 1)).astype(BF16)
        within = sum(jnp.dot(part, strict, preferred_element_type=F32) for part in _split3_bf16(lf))
        tot = jnp.sum(lf, axis=-1, keepdims=True)
        later = run_sc[...]
        c_new = cnew_sc[...]
        qf = qf_ref[...]
        ql = ql_ref[...]
        qr = qr_ref[...]
        s_f, s_m, vts, cms = [None] * pp, [None] * pp, [], []
        for j in reversed(range(pp)):
            d = within[j * FOX_HEADS:(j + 1) * FOX_HEADS] + later
            later = later + tot[j * FOX_HEADS:(j + 1) * FOX_HEADS]
            kt = kbuf[slot, j].reshape(FOX_KV_WIDTH, page).astype(BF16)
            cm = cbuf[slot, j].astype(BF16)
            rt = rbuf[slot, j].astype(BF16)
            s_f[j] = (jnp.dot(qf, kt, preferred_element_type=F32)
                      + (jnp.concatenate([d] * dec_seq, axis=0) + c_new))
            s_m[j] = (lax.dot_general(ql, cm, NT_DIMS, preferred_element_type=F32)
                      + jnp.dot(qr, rt, preferred_element_type=F32))
        run_sc[...] = later
        s = jnp.concatenate([jnp.concatenate(s_f, axis=1), jnp.concatenate(s_m, axis=1)], axis=0)

        def pv(p):
            o_f = jnp.zeros((nr, FOX_KV_WIDTH), F32)
            o_m = jnp.zeros((nr, MLA_KV_RANK), F32)
            for j in range(pp):
                pj = p[:, j * page:(j + 1) * page].astype(BF16)
                vt = vbuf[slot, j].reshape(FOX_KV_WIDTH, page).astype(BF16)
                o_f = o_f + lax.dot_general(pj[0:nr], vt, NT_DIMS, preferred_element_type=F32)
                o_m = o_m + jnp.dot(pj[nr:2 * nr], cbuf[slot, j].astype(BF16), preferred_element_type=F32)
            return o_f, o_m

        attend_next(s, pv)

    @pl.when(step == nch)
    def _():
        l = l_sc[...]
        of_ref[...] = accf_sc[...] / l[0:nr]
        om_ref[...] = accm_sc[...] / l[nr:2 * nr]


def _sample_attention(layer, page_table, qf, ql, qr, kn, vn, cn, rn, lfn, kt, vt, cc, rt, lt):
    n_seq, n_pages = page_table.shape
    page = cc.shape[2]
    nr = qf.shape[1]
    dec_seq = nr // FOX_HEADS
    pp = PAGES_PER_STEP
    ring = PAGE_RING_SLOTS
    nch = n_pages // pp
    per_seq = lambda *tail: pl.BlockSpec((None,) + tail, lambda b, s, pt: (b,) + (0,) * len(tail))
    hbm = pl.BlockSpec(memory_space=pl.ANY)
    kern = functools.partial(_sample_attn_kernel, layer=layer, n_pages=n_pages, n_seq=n_seq, page=page,
                             dec_seq=dec_seq)
    return pl.pallas_call(
        kern,
        grid_spec=pltpu.PrefetchScalarGridSpec(
            num_scalar_prefetch=1,
            grid=(n_seq, nch + 1),
            in_specs=[per_seq(nr, FOX_KV_WIDTH), per_seq(nr, MLA_KV_RANK), per_seq(nr, MLA_ROPE_DIM),
                      per_seq(page, FOX_KV_WIDTH), per_seq(page, FOX_KV_WIDTH), per_seq(page, MLA_KV_RANK),
                      per_seq(page, MLA_ROPE_DIM), per_seq(FOX_HEADS, page),
                      hbm, hbm, hbm, hbm, hbm],
            out_specs=[per_seq(nr, FOX_KV_WIDTH), per_seq(nr, MLA_KV_RANK)],
            scratch_shapes=[
                pltpu.VMEM((ring, pp, FOX_KV_HEADS, HEAD_DIM, page), F32),
                pltpu.VMEM((ring, pp, FOX_KV_HEADS, HEAD_DIM, page), F32),
                pltpu.VMEM((ring, pp, page, MLA_KV_RANK), F32), pltpu.VMEM((ring, pp, MLA_ROPE_DIM, page), F32),
                pltpu.VMEM((ring, pp, FOX_HEADS, page), F32), pltpu.SemaphoreType.DMA((5, ring)),
                pltpu.VMEM((2 * nr, 1), F32), pltpu.VMEM((2 * nr, 1), F32),
                pltpu.VMEM((nr, FOX_KV_WIDTH), F32), pltpu.VMEM((nr, MLA_KV_RANK), F32),
                pltpu.VMEM((FOX_HEADS, 1), F32), pltpu.VMEM((nr, 1), F32)],
        ),
        out_shape=[jax.ShapeDtypeStruct((n_seq, nr, FOX_KV_WIDTH), F32),
                   jax.ShapeDtypeStruct((n_seq, nr, MLA_KV_RANK), F32)],
        compiler_params=_params("arbitrary", "arbitrary"),
        name="sample_attn",
    )(page_table.reshape(-1), qf, ql, qr, kn, vn, cn, rn, lfn, kt, vt, cc, rt, lt)


def _merge_kernel(fo_ref, lat_ref, x_ref, gf_ref, gm_ref, wuv_ref, wo_ref, o_ref):
    fn = _rms(fo_ref[...], gf_ref[...])
    mo = jnp.dot(lat_ref[...].astype(BF16), wuv_ref[...], preferred_element_type=F32)
    mn = _rms(mo, gm_ref[...])
    y = (jnp.dot(fn.astype(BF16), wo_ref[0:FOX_WIDTH, :], preferred_element_type=F32)
         + jnp.dot(mn.astype(BF16), wo_ref[FOX_WIDTH:FOX_WIDTH + MLA_WIDTH, :], preferred_element_type=F32))
    o_ref[...] = x_ref[...] + y


def _merge(fo, lat, x, gf, gm, wuv, wo, tm):
    t = x.shape[0]
    row = lambda n: pl.BlockSpec((tm, n), lambda i: (i, 0))
    full = lambda a: pl.BlockSpec(a.shape, lambda i: (0,) * a.ndim)
    return pl.pallas_call(
        _merge_kernel,
        grid=(t // tm,),
        in_specs=[row(FOX_WIDTH), row(MLA_HEADS * MLA_KV_RANK), row(D_MODEL), full(gf), full(gm),
                  full(wuv), full(wo)],
        out_specs=row(D_MODEL),
        out_shape=jax.ShapeDtypeStruct((t, D_MODEL), F32),
        compiler_params=_params("parallel"),
        name="head_merge",
    )(fo, lat, x, gf, gm, wuv, wo)


def _swiglu_chunk(h, wg, wu, wd):
    gate = jnp.dot(h, wg, preferred_element_type=F32)
    up = jnp.dot(h, wu, preferred_element_type=F32)
    a = gate * (1.0 / (1.0 + jnp.exp(-gate))) * up
    return jnp.dot(a.astype(BF16), wd, preferred_element_type=F32)


def _ffn_kernel(x_ref, g_ref, wg_ref, wu_ref, wd_ref, o_ref, h_sc, acc_sc):
    f = pl.program_id(1)

    @pl.when(f == 0)
    def _():
        h_sc[...] = _rms(x_ref[...], g_ref[...]).astype(BF16)
        acc_sc[...] = jnp.zeros_like(acc_sc)

    acc_sc[...] += _swiglu_chunk(h_sc[...], wg_ref[...], wu_ref[...], wd_ref[...])

    @pl.when(f == pl.num_programs(1) - 1)
    def _():
        o_ref[...] = x_ref[...] + acc_sc[...]


def _ffn_dense(x, g, wg, wu, wd, tf, tm):
    t = x.shape[0]
    nf = wg.shape[1] // tf
    return pl.pallas_call(
        _ffn_kernel,
        grid=(t // tm, nf),
        in_specs=[pl.BlockSpec((tm, D_MODEL), lambda i, f: (i, 0)),
                  pl.BlockSpec((1, D_MODEL), lambda i, f: (0, 0)),
                  pl.BlockSpec((D_MODEL, tf), lambda i, f: (0, f)),
                  pl.BlockSpec((D_MODEL, tf), lambda i, f: (0, f)),
                  pl.BlockSpec((tf, D_MODEL), lambda i, f: (f, 0))],
        out_specs=pl.BlockSpec((tm, D_MODEL), lambda i, f: (i, 0)),
        out_shape=jax.ShapeDtypeStruct((t, D_MODEL), F32),
        scratch_shapes=[pltpu.VMEM((tm, D_MODEL), BF16), pltpu.VMEM((tm, D_MODEL), F32)],
        compiler_params=_params("parallel", "arbitrary"),
        name="ffn_dense",
    )(x, g, wg, wu, wd)


def _router_kernel(x_ref, g_ref, wr_ref, idx_ref, w_ref):
    h = _rms(x_ref[...], g_ref[...])
    logits = jnp.dot(h, wr_ref[...], precision=lax.Precision.HIGHEST, preferred_element_type=F32)
    lane = lax.broadcasted_iota(I32, logits.shape, 1)
    v1 = jnp.max(logits, axis=-1, keepdims=True)
    i1 = jnp.min(jnp.where(logits == v1, lane, N_EXPERTS), axis=-1, keepdims=True)
    rest = jnp.where(lane == i1, -jnp.inf, logits)
    v2 = jnp.max(rest, axis=-1, keepdims=True)
    i2 = jnp.min(jnp.where(rest == v2, lane, N_EXPERTS), axis=-1, keepdims=True)
    e2 = jnp.exp(v2 - v1)
    first = lax.broadcasted_iota(I32, idx_ref.shape, 1) == 0
    idx_ref[...] = jnp.where(first, i1, i2)
    w_ref[...] = jnp.where(first, 1.0 / (1.0 + e2), e2 / (1.0 + e2))


def _router(x, g, wr, tm):
    t = x.shape[0]
    return pl.pallas_call(
        _router_kernel,
        grid=(t // tm,),
        in_specs=[pl.BlockSpec((tm, D_MODEL), lambda i: (i, 0)),
                  pl.BlockSpec((1, D_MODEL), lambda i: (0, 0)),
                  pl.BlockSpec((D_MODEL, N_EXPERTS), lambda i: (0, 0))],
        out_specs=[pl.BlockSpec((tm, TOP_K), lambda i: (i, 0)), pl.BlockSpec((tm, TOP_K), lambda i: (i, 0))],
        out_shape=[jax.ShapeDtypeStruct((t, TOP_K), I32), jax.ShapeDtypeStruct((t, TOP_K), F32)],
        compiler_params=_params("parallel"),
        name="moe_router",
    )(x, g, wr)


MOE_ROW_TILE = 512
CHUNKS = D_MODEL // LANES


def _route_tables(idx, w, tmg):
    t = idx.shape[0]
    na = TOP_K * t
    a_exp = idx.reshape(na)
    onehot = (a_exp[:, None] == jnp.arange(N_EXPERTS, dtype=I32)[None, :]).astype(I32)
    csum = jnp.cumsum(onehot, axis=0)
    rank = jnp.take_along_axis(csum, a_exp[:, None], axis=1)[:, 0] - 1
    counts = csum[-1]
    tiles_per = (counts + tmg - 1) // tmg
    tile_end = jnp.cumsum(tiles_per)
    group_start = (tile_end - tiles_per) * tmg
    slot = group_start[a_exp] + rank
    nt = -(-na // tmg) + N_EXPERTS
    nslot = nt * tmg
    slot_a = jnp.full((nslot,), -1, I32).at[slot].set(jnp.arange(na, dtype=I32))
    real = slot_a >= 0
    a = jnp.maximum(slot_a, 0)
    slot_token = a // TOP_K
    slot_dst = (a % TOP_K) * t + slot_token
    slot_w = jnp.where(real, w.reshape(na)[a], 0.0)
    n_used = tile_end[-1]
    tile_ids = jnp.arange(nt, dtype=I32)
    tile_expert = jnp.minimum(jnp.searchsorted(tile_end, jnp.minimum(tile_ids, n_used - 1), side='right'),
                              N_EXPERTS - 1).astype(I32)
    tile_valid = (tile_ids < n_used).astype(I32)
    tile_count = jnp.sum(real.reshape(nt, tmg), axis=1).astype(I32)
    return (tile_expert, tile_valid, tile_count, slot_token.reshape(nt, 1, tmg), slot_dst.reshape(nt, 1, tmg),
            slot_w.reshape(nslot, 1))


def _moe_gmm_kernel(te_ref, tv_ref, tc_ref, tok_ref, ntok_ref, dst_ref, x_hbm, g_ref, w_ref, wg_ref, wu_ref, wd_ref,
                    y_hbm, xbuf, ybuf, h_sc, acc_sc, gsem, ssem, *, tmg):
    i = pl.program_id(0)
    f = pl.program_id(1)
    nt = pl.num_programs(0)
    nf = pl.num_programs(1)
    slot = lax.rem(i, 2)

    def start_gather(tref, sl):
        def body(r, c):
            pltpu.make_async_copy(x_hbm.at[tref[0, r]], xbuf.at[sl, pl.ds(r * CHUNKS, CHUNKS)], gsem.at[sl]).start()
            return c
        lax.fori_loop(0, tmg, body, 0, unroll=8)

    def wait_gather(sl):
        pltpu.make_async_copy(xbuf.at[sl], xbuf.at[sl], gsem.at[sl]).wait()

    def scatter(n, start):
        def body(r, c):
            d = dst_ref[0, r] if start else 0
            cp = pltpu.make_async_copy(ybuf.at[pl.ds(r * CHUNKS, CHUNKS)], y_hbm.at[d], ssem.at[0])
            cp.start() if start else cp.wait()
            return c
        lax.fori_loop(0, n, body, 0)

    start_scatter = lambda n: scatter(n, True)
    wait_scatter = lambda n: scatter(n, False)

    @pl.when(f == 0)
    def _():
        @pl.when(i == 0)
        def _():
            start_gather(tok_ref, 0)

        wait_gather(slot)

        @pl.when(i + 1 < nt)
        def _():
            start_gather(ntok_ref, 1 - slot)

    valid = tv_ref[i] == 1

    @pl.when(valid & (f == 0))
    def _():
        xs = [xbuf[slot, pl.ds(c, tmg, stride=CHUNKS), :] for c in range(CHUNKS)]
        ss = sum(jnp.sum(xc * xc, axis=-1, keepdims=True) for xc in xs)
        rs = lax.rsqrt(ss * (1.0 / D_MODEL) + NORM_EPS)
        for c in range(CHUNKS):
            h_sc[:, c * LANES:(c + 1) * LANES] = (xs[c] * rs * g_ref[:, c * LANES:(c + 1) * LANES]).astype(BF16)
        acc_sc[...] = jnp.zeros_like(acc_sc)

    @pl.when(valid)
    def _():
        acc_sc[...] += _swiglu_chunk(h_sc[...], wg_ref[...], wu_ref[...], wd_ref[...])

    @pl.when(valid & (f == nf - 1))
    def _():
        @pl.when(i > 0)
        def _():
            wait_scatter(tc_ref[jnp.maximum(i - 1, 0)])

        y = acc_sc[...] * w_ref[...]
        for c in range(CHUNKS):
            ybuf[pl.ds(c, tmg, stride=CHUNKS), :] = y[:, c * LANES:(c + 1) * LANES]
        start_scatter(tc_ref[i])

        last_valid = jnp.where(i + 1 < nt, tv_ref[jnp.minimum(i + 1, nt - 1)], 0) == 0

        @pl.when(last_valid)
        def _():
            wait_scatter(tc_ref[i])


def _moe_gmm(x3, g, tables, wg, wu, wd, tf, tmg):
    tile_expert, tile_valid, tile_count, slot_token, slot_dst, slot_w = tables
    t = x3.shape[0]
    nt = slot_token.shape[0]
    ne, _, ff = wg.shape
    nf = ff // tf
    smem_tile = lambda fn: pl.BlockSpec((None, 1, tmg), fn, memory_space=pltpu.SMEM)
    return pl.pallas_call(
        functools.partial(_moe_gmm_kernel, tmg=tmg),
        grid_spec=pltpu.PrefetchScalarGridSpec(
            num_scalar_prefetch=3,
            grid=(nt, nf),
            in_specs=[smem_tile(lambda i, f, te, tv, tc: (i, 0, 0)),
                      smem_tile(lambda i, f, te, tv, tc: (jnp.minimum(i + 1, nt - 1), 0, 0)),
                      smem_tile(lambda i, f, te, tv, tc: (i, 0, 0)),
                      pl.BlockSpec(memory_space=pl.ANY),
                      pl.BlockSpec((1, D_MODEL), lambda i, f, te, tv, tc: (0, 0)),
                      pl.BlockSpec((tmg, 1), lambda i, f, te, tv, tc: (i, 0)),
                      pl.BlockSpec((None, D_MODEL, tf), lambda i, f, te, tv, tc: (te[i], 0, f)),
                      pl.BlockSpec((None, D_MODEL, tf), lambda i, f, te, tv, tc: (te[i], 0, f)),
                      pl.BlockSpec((None, tf, D_MODEL), lambda i, f, te, tv, tc: (te[i], f, 0))],
            out_specs=pl.BlockSpec(memory_space=pl.ANY),
            scratch_shapes=[pltpu.VMEM((2, tmg * CHUNKS, LANES), F32), pltpu.VMEM((tmg * CHUNKS, LANES), F32),
                            pltpu.VMEM((tmg, D_MODEL), BF16), pltpu.VMEM((tmg, D_MODEL), F32),
                            pltpu.SemaphoreType.DMA((2,)), pltpu.SemaphoreType.DMA((1,))],
        ),
        out_shape=jax.ShapeDtypeStruct((TOP_K * t, CHUNKS, LANES), F32),
        compiler_params=_params("arbitrary", "arbitrary"),
        name="moe_experts",
    )(tile_expert, tile_valid, tile_count, slot_token, slot_token, slot_dst, x3, g, slot_w, wg, wu, wd)


def _moe_combine_kernel(x_ref, y0_ref, y1_ref, o_ref):
    o_ref[...] = (x_ref[...] + y0_ref[...]) + y1_ref[...]


def _moe_combine(x3, y3, tm):
    t = x3.shape[0]
    nb = t // tm
    blk = lambda fn: pl.BlockSpec((tm, CHUNKS, LANES), fn)
    return pl.pallas_call(
        _moe_combine_kernel,
        grid=(nb,),
        in_specs=[blk(lambda i: (i, 0, 0)), blk(lambda i: (i, 0, 0)), blk(lambda i: (i + nb, 0, 0))],
        out_specs=blk(lambda i: (i, 0, 0)),
        out_shape=jax.ShapeDtypeStruct(x3.shape, F32),
        compiler_params=_params("parallel"),
        name="moe_combine",
    )(x3, y3, y3)


def _final_norm_kernel(x_ref, g_ref, o_ref):
    o_ref[...] = _rms(x_ref[...], g_ref[...])


def _final_norm(x, g, tm):
    t = x.shape[0]
    return pl.pallas_call(
        _final_norm_kernel,
        grid=(t // tm,),
        in_specs=[pl.BlockSpec((tm, D_MODEL), lambda i: (i, 0)), pl.BlockSpec((1, D_MODEL), lambda i: (0, 0))],
        out_specs=pl.BlockSpec((tm, D_MODEL), lambda i: (i, 0)),
        out_shape=jax.ShapeDtypeStruct((t, D_MODEL), F32),
        compiler_params=_params("parallel"),
        name="final_norm",
    )(x, g)


def _rotate_half_cols(w):
    half = w.shape[-1] // 2
    return jnp.concatenate([-w[..., half:], w[..., :half]], axis=-1)


def _prep_w_in(w):
    ends = [FOX_WIDTH, FOX_WIDTH + FOX_KV_WIDTH, FOX_WIDTH + 2 * FOX_KV_WIDTH]
    ends.append(ends[-1] + FOX_HEADS)
    ends.append(ends[-1] + MLA_Q_RANK)
    ends.append(ends[-1] + MLA_KV_RANK)
    fq, fk, fv, fz, cq, ckv, kr = jnp.split(w, ends, axis=-1)
    pad = jnp.zeros((w.shape[0], LANES - 2 * MLA_ROPE_DIM - FOX_HEADS), w.dtype)
    return jnp.concatenate([fq * (FOX_SCALE * LOG2E), fk, fv, cq, ckv, kr, _rotate_half_cols(kr), fz, pad],
                           axis=-1).astype(BF16)


def _prep_w_uq(w_uq):
    nope = w_uq[:, :, :MLA_NOPE_DIM]
    rope = w_uq[:, :, MLA_NOPE_DIM:]
    padto = lambda a: jnp.pad(a, ((0, 0), (0, 0), (0, LANES - a.shape[-1])))
    w = jnp.concatenate([padto(nope), padto(rope), padto(_rotate_half_cols(rope))], axis=-1)
    return jnp.transpose(w, (1, 0, 2)).astype(BF16)


def _prep_w_uk(w_uk):
    w = jnp.transpose(w_uk, (1, 2, 0))
    return jnp.pad(w, ((0, 0), (0, LANES - w.shape[1]), (0, 0))).astype(BF16)


def _prep_w_uv(w_uv):
    c, h, v = w_uv.shape
    bd = jnp.einsum('chv,hg->hcgv', w_uv, jnp.eye(h, dtype=w_uv.dtype))
    return bd.reshape(h * c, h * v).astype(BF16)


def _rope_tables(pos):
    half = MLA_ROPE_DIM // 2
    inv_freq = jnp.power(ROPE_THETA, -jnp.arange(half, dtype=F32) / half)
    ang = pos.astype(F32)[:, None] * inv_freq[None, :]
    pad = lambda a: jnp.pad(jnp.concatenate([a, a], axis=-1), ((0, 0), (0, LANES - MLA_ROPE_DIM)))
    return pad(jnp.cos(ang)), pad(jnp.sin(ang))


def kernel(x_prompt, x_sample, cache_fox_k, cache_fox_v, cache_fox_logf, cache_mla_ckv, cache_mla_krope, page_table, w_in, b_f, g_attn, g_cq, g_ckv, w_uq, w_uk, w_uv, g_fox_out, g_mla_out, w_o, g_ffn, w_gate_dense, w_up_dense, w_down_dense, w_router, w_gate_exp, w_up_exp, w_down_exp, g_final):
    batch, seq, _ = x_prompt.shape
    n_seq, dec_seq, _ = x_sample.shape
    depth, n_pool, page = cache_fox_k.shape[:3]
    n_pages = page_table.shape[1]
    past_len = n_pages * page
    tp = batch * seq
    ts = n_seq * dec_seq
    t_all = tp + ts
    tm = _largest_tile(TOKEN_TILE, tp, ts)
    tk = _largest_tile(512, seq)
    tmg = _largest_tile(MOE_ROW_TILE, t_all)
    nr = dec_seq * FOX_HEADS
    assert n_pages % PAGES_PER_STEP == 0 and dec_seq <= page

    x = jnp.concatenate([x_prompt.reshape(tp, D_MODEL), x_sample.reshape(ts, D_MODEL)], axis=0)
    pos = jnp.concatenate([jnp.tile(jnp.arange(seq, dtype=I32), batch),
                           past_len + jnp.tile(jnp.arange(dec_seq, dtype=I32), n_seq)])
    cos, sin = _rope_tables(pos)

    kt = jnp.transpose(cache_fox_k, (0, 1, 3, 4, 2))
    vt = jnp.transpose(cache_fox_v, (0, 1, 3, 4, 2))
    rt = jnp.transpose(cache_mla_krope, (0, 1, 3, 2))
    lt = jnp.transpose(cache_fox_logf, (0, 1, 3, 2))

    row = lambda a: a.reshape(1, -1)
    outs = [[] for _ in range(10)]
    eye_kv = jnp.eye(FOX_KV_HEADS, dtype=BF16)
    pad_rows = lambda a: jnp.pad(a.reshape(n_seq, dec_seq, -1), ((0, 0), (0, page - dec_seq), (0, 0)))

    for l in range(depth):
        fq, fk, fv, fkb, fvb, cq, ckv, kr, kcat, lf = _proj(
            x, row(g_attn[l]), _prep_w_in(w_in[l]), row(g_ckv[l]), row(b_f[l]), cos, sin, tm)
        qcat = _mla_q(cq, row(g_cq[l]), _prep_w_uq(w_uq[l]), _prep_w_uk(w_uk[l]), cos, sin, tm)

        c = _cumsum(lf[:tp].reshape(batch, seq, FOX_HEADS), tk)
        c_t = jnp.transpose(c, (0, 2, 1))
        fo_p = _fox_prompt_attention(fq, fkb, fvb, c_t[:, :, :, None], c_t[:, :, None, :], batch, seq,
                                     _largest_tile(FOX_Q_TILE, seq), tk)
        lat_p = _mla_prompt_attention(qcat, kcat, batch, seq, _largest_tile(MLA_Q_TILE, seq), tk)

        q5 = fq[tp:].reshape(n_seq, dec_seq, FOX_KV_HEADS, FOX_GROUP, HEAD_DIM)
        qf = jnp.einsum('bqkgd,kj->bqkgjd', q5, eye_kv).reshape(n_seq, nr, FOX_KV_WIDTH)
        qm = jnp.transpose(qcat[:, tp:].reshape(MLA_HEADS, n_seq, dec_seq, MLA_QK_DIM),
                           (1, 2, 0, 3)).reshape(n_seq, nr, MLA_QK_DIM)
        lfn = jnp.transpose(pad_rows(lf[tp:]), (0, 2, 1))
        of, om = _sample_attention(l, page_table, qf, qm[:, :, :MLA_KV_RANK], qm[:, :, MLA_KV_RANK:],
                                   pad_rows(fk[tp:]), pad_rows(fv[tp:]), pad_rows(ckv[tp:]), pad_rows(kr[tp:]),
                                   lfn, kt, vt, cache_mla_ckv, rt, lt)
        of6 = of.reshape(n_seq, dec_seq, FOX_KV_HEADS, FOX_GROUP, FOX_KV_HEADS, HEAD_DIM)
        fo_s = jnp.einsum('bqkgjd,kj->bqkgd', of6, jnp.eye(FOX_KV_HEADS, dtype=F32)).reshape(ts, FOX_WIDTH)
        lat_s = om.reshape(ts, MLA_HEADS * MLA_KV_RANK)

        x = _merge(jnp.concatenate([fo_p, fo_s], axis=0), jnp.concatenate([lat_p, lat_s], axis=0), x,
                   row(g_fox_out[l]), row(g_mla_out[l]), _prep_w_uv(w_uv[l]), w_o[l].astype(BF16), tm)

        i = l // 2
        if l % 2 == 0:
            x = _ffn_dense(x, row(g_ffn[l]), w_gate_dense[i].astype(BF16), w_up_dense[i].astype(BF16),
                           w_down_dense[i].astype(BF16), _ff_tile(w_gate_dense.shape[-1]), tm)
        else:
            idx, wts = _router(x, row(g_ffn[l]), w_router[i], tm)
            x3 = x.reshape(t_all, CHUNKS, LANES)
            y3 = _moe_gmm(x3, row(g_ffn[l]), _route_tables(idx, wts, tmg), w_gate_exp[i].astype(BF16),
                          w_up_exp[i].astype(BF16), w_down_exp[i].astype(BF16),
                          _ff_tile(w_gate_exp.shape[-1]), tmg)
            x = _moe_combine(x3, y3, tmg).reshape(t_all, D_MODEL)

        for dst, a in zip(outs, (fk[:tp], fv[:tp], lf[:tp], ckv[:tp], kr[:tp],
                                 fk[tp:], fv[tp:], lf[tp:], ckv[tp:], kr[tp:])):
            dst.append(a)

    y = _final_norm(x, row(g_final), tm)
    shapes = [(batch, seq, FOX_KV_HEADS, HEAD_DIM), (batch, seq, FOX_KV_HEADS, HEAD_DIM), (batch, seq, FOX_HEADS),
              (batch, seq, MLA_KV_RANK), (batch, seq, MLA_ROPE_DIM),
              (n_seq, dec_seq, FOX_KV_HEADS, HEAD_DIM), (n_seq, dec_seq, FOX_KV_HEADS, HEAD_DIM),
              (n_seq, dec_seq, FOX_HEADS), (n_seq, dec_seq, MLA_KV_RANK), (n_seq, dec_seq, MLA_ROPE_DIM)]
    caches = tuple(jnp.stack(o).reshape((depth,) + s) for o, s in zip(outs, shapes))
    return (y[:tp].reshape(batch, seq, D_MODEL), y[tp:].reshape(n_seq, dec_seq, D_MODEL)) + caches
```
